```python
import math
import jax, jax.numpy as jnp
from jax import lax
import numpy as np

D_MODEL = 1024
BATCH = 2
SEQ = 16384
DEPTH = 4

CTX_LEN = 256
GRID_W = 64
N_MIXERS = 2
N_ATTN_LAYERS = (DEPTH + N_MIXERS - 1) // N_MIXERS
N_CMLP_LAYERS = DEPTH // N_MIXERS
DA_HEADS = 8
DA_QK_DIM = 64
DA_V_DIM = 2 * DA_QK_DIM
Q_BLOCK = 128
ROPE_BASE = 10000.0
ROPE_AXIS_DIM = DA_QK_DIM // 2
QKV_WIDTH = 2 * DA_HEADS * DA_QK_DIM * 2 + DA_HEADS * DA_V_DIM
CHUNK = 128
CM_WIDTH = D_MODEL
CM_GROUPS = 8
CM_GROUP_DIM = CM_WIDTH // CM_GROUPS
FFN_HIDDEN = ((math.ceil(8 * D_MODEL / 3) + 255) // 256) * 256
N_MOD = 6
EPS = 1e-6

kernel_name = 'hybrid_diffattn_chunkmlp_dit'


def rms_norm(x, g):
    xf = x.astype(jnp.float32)
    y = xf * lax.rsqrt(jnp.mean(xf * xf, axis=-1, keepdims=True) + EPS)
    return (y * g.astype(jnp.float32)).astype(x.dtype)


def layer_norm(x, g, b):
    xf = x.astype(jnp.float32)
    mu = jnp.mean(xf, axis=-1, keepdims=True)
    var = jnp.mean(jnp.square(xf - mu), axis=-1, keepdims=True)
    y = (xf - mu) * lax.rsqrt(var + EPS) * g.astype(jnp.float32) + b.astype(jnp.float32)
    return y.astype(x.dtype)


def modulate(h, shift, scale):
    return h * (1 + scale) + shift


def axial_angles(n_tokens):
    rows = n_tokens // GRID_W
    row = jnp.broadcast_to(jnp.arange(rows, dtype=jnp.float32)[:, None], (rows, GRID_W)).reshape(-1)
    col = jnp.broadcast_to(jnp.arange(GRID_W, dtype=jnp.float32)[None, :], (rows, GRID_W)).reshape(-1)
    inv_freq = ROPE_BASE ** (-jnp.arange(0, ROPE_AXIS_DIM, 2, dtype=jnp.float32) / ROPE_AXIS_DIM)
    return row[:, None] * inv_freq, col[:, None] * inv_freq


def rotate(x, ang):
    cos = jnp.cos(ang)[None, :, None, :].astype(x.dtype)
    sin = jnp.sin(ang)[None, :, None, :].astype(x.dtype)
    x1, x2 = jnp.split(x, 2, axis=-1)
    return jnp.concatenate([x1 * cos - x2 * sin, x2 * cos + x1 * sin], axis=-1)


def rope_2d(x, ang_row, ang_col):
    x_row, x_col = jnp.split(x, 2, axis=-1)
    return jnp.concatenate([rotate(x_row, ang_row), rotate(x_col, ang_col)], axis=-1)


def diff_attend(q, k, v, lam):
    b, sq, h2, dk = q.shape
    n_blk = sq // Q_BLOCK
    q_blocks = jnp.moveaxis(q.reshape(b, n_blk, Q_BLOCK, h2, dk), 1, 0)
    scale = dk ** -0.5

    def one_block(qb):
        s = jnp.einsum('bqhd,bkhd->bhqk', qb, k, preferred_element_type=jnp.float32) * scale
        p = jax.nn.softmax(s, axis=-1).reshape(b, h2 // 2, 2, Q_BLOCK, -1)
        w = (p[:, :, 0] - lam * p[:, :, 1]).astype(v.dtype)
        return jnp.einsum('bhqk,bkhe->bqhe', w, v)

    out = lax.map(one_block, q_blocks)
    return jnp.moveaxis(out, 0, 1).reshape(b, sq, h2 // 2, -1)


def diff_attention(hx, hc, j, layer_idx, ang_row, ang_col, ctx_live,
                   w_qkv, w_attn_out, lam_q1, lam_k1, lam_q2, lam_k2, g_subln):
    lam_init = 0.8 - 0.6 * math.exp(-0.3 * layer_idx)
    f32 = jnp.float32
    lam = (jnp.exp(jnp.sum(lam_q1[j].astype(f32) * lam_k1[j].astype(f32)))
           - jnp.exp(jnp.sum(lam_q2[j].astype(f32) * lam_k2[j].astype(f32))) + lam_init)
    qk_w = 2 * DA_HEADS * DA_QK_DIM

    def project(h):
        b, s, _ = h.shape
        q, k, v = jnp.split(h @ w_qkv[j], [qk_w, 2 * qk_w], axis=-1)
        return (q.reshape(b, s, 2 * DA_HEADS, DA_QK_DIM),
                k.reshape(b, s, 2 * DA_HEADS, DA_QK_DIM),
                v.reshape(b, s, DA_HEADS, DA_V_DIM))

    def finish(o):
        b, s = o.shape[:2]
        o = rms_norm(o, g_subln[j]) * (1 - lam_init)
        return o.reshape(b, s, DA_HEADS * DA_V_DIM) @ w_attn_out[j]

    qx, kx, vx = project(hx)
    qx = rope_2d(qx, ang_row, ang_col)
    kx = rope_2d(kx, ang_row, ang_col)
    qc, kc, vc = project(hc)
    k_all = jnp.concatenate([kc, kx], axis=1)
    v_all = jnp.concatenate([vc, vx], axis=1)
    yx = finish(diff_attend(qx, k_all, v_all, lam))
    yc = finish(diff_attend(qc, kc, vc, lam)) if ctx_live else None
    return yx, yc


def chunk_mlp(h, j, w_uv, b_uv, g_sgu, b_sgu, w_spatial, b_spatial, w_cmlp_out):
    b, s, _ = h.shape
    z = jax.nn.gelu(h @ w_uv[j] + b_uv[j], approximate=False)
    u, v = jnp.split(z, 2, axis=-1)
    v = layer_norm(v, g_sgu[j], b_sgu[j])
    v = v.reshape(b, s // CHUNK, CHUNK, CM_GROUPS, CM_GROUP_DIM)
    sv = (jnp.einsum('gpq,bnqge->bnpge', w_spatial[j], v)
          + b_spatial[j].T[None, None, :, :, None])
    return (u * sv.reshape(b, s, CM_WIDTH)) @ w_cmlp_out[j]


def swiglu(h, j, w_ffn_in, w_ffn_out):
    gate, up = jnp.split(h @ w_ffn_in[j], 2, axis=-1)
    return (jax.nn.silu(gate) * up) @ w_ffn_out[j]


def setup_inputs(seed: int = 0) -> dict:
    key = jax.random.key(seed)
    ks = jax.random.split(key, 26)
    nrm = jax.random.normal
    f32 = jnp.float32
    D, NA, NB, F = D_MODEL, N_ATTN_LAYERS, N_CMLP_LAYERS, FFN_HIDDEN
    return {
        'x': nrm(ks[0], (BATCH, SEQ, D), f32),
        'c': nrm(ks[1], (BATCH, D), f32),
        'ctx': nrm(ks[2], (BATCH, CTX_LEN, D), f32),
        'c_ctx': nrm(ks[3], (D,), f32),
        'w_ada': nrm(ks[4], (DEPTH, D, N_MOD * D), f32) * (0.5 * D ** -0.5),
        'b_ada': nrm(ks[5], (DEPTH, N_MOD * D), f32) * 0.01,
        'g_mix_pre': 1.0 + 0.05 * nrm(ks[6], (DEPTH, D), f32),
        'g_mix_post': 1.0 + 0.05 * nrm(ks[7], (DEPTH, D), f32),
        'g_ffn_pre': 1.0 + 0.05 * nrm(ks[8], (DEPTH, D), f32),
        'g_ffn_post': 1.0 + 0.05 * nrm(ks[9], (DEPTH, D), f32),
        'w_qkv': nrm(ks[10], (NA, D, QKV_WIDTH), f32) * D ** -0.5,
        'w_attn_out': nrm(ks[11], (NA, DA_HEADS * DA_V_DIM, D), f32) * (DA_HEADS * DA_V_DIM) ** -0.5,
        'lam_q1': 0.1 * nrm(ks[12], (NA, DA_QK_DIM), f32),
        'lam_k1': 0.1 * nrm(ks[13], (NA, DA_QK_DIM), f32),
        'lam_q2': 0.1 * nrm(ks[14], (NA, DA_QK_DIM), f32),
        'lam_k2': 0.1 * nrm(ks[15], (NA, DA_QK_DIM), f32),
        'g_subln': 1.0 + 0.05 * nrm(ks[16], (NA, DA_V_DIM), f32),
        'w_uv': nrm(ks[17], (NB, D, 2 * CM_WIDTH), f32) * D ** -0.5,
        'b_uv': 0.02 * nrm(ks[18], (NB, 2 * CM_WIDTH), f32),
        'g_sgu': 1.0 + 0.05 * nrm(ks[19], (NB, CM_WIDTH), f32),
        'b_sgu': 0.02 * nrm(ks[20], (NB, CM_WIDTH), f32),
        'w_spatial': nrm(ks[21], (NB, CM_GROUPS, CHUNK, CHUNK), f32) * (0.5 * CHUNK ** -0.5),
        'b_spatial': 1.0 + 0.1 * nrm(ks[22], (NB, CM_GROUPS, CHUNK), f32),
        'w_cmlp_out': nrm(ks[23], (NB, CM_WIDTH, D), f32) * CM_WIDTH ** -0.5,
        'w_ffn_in': nrm(ks[24], (DEPTH, D, 2 * F), f32) * D ** -0.5,
        'w_ffn_out': nrm(ks[25], (DEPTH, F, D), f32) * F ** -0.5,
    }


def reference(x, c, ctx, c_ctx, w_ada, b_ada, g_mix_pre, g_mix_post, g_ffn_pre, g_ffn_post,
              w_qkv, w_attn_out, lam_q1, lam_k1, lam_q2, lam_k2, g_subln,
              w_uv, b_uv, g_sgu, b_sgu, w_spatial, b_spatial, w_cmlp_out,
              w_ffn_in, w_ffn_out):
    ang_row, ang_col = axial_angles(x.shape[1])
    silu_c = jax.nn.silu(c)
    silu_cc = jax.nn.silu(c_ctx)
    last_attn = max(i for i in range(DEPTH) if i % N_MIXERS == 0)
    for i in range(DEPTH):
        is_attn = (i % N_MIXERS == 0)
        j = i // N_MIXERS
        ctx_live = i < last_attn
        mod_x = (silu_c @ w_ada[i] + b_ada[i])[:, None, :]
        mod_c = silu_cc @ w_ada[i] + b_ada[i]
        sh_m, sc_m, g_m, sh_f, sc_f, g_f = jnp.split(mod_x, N_MOD, axis=-1)
        csh_m, csc_m, cg_m, csh_f, csc_f, cg_f = jnp.split(mod_c, N_MOD, axis=-1)

        hx = modulate(rms_norm(x, g_mix_pre[i]), sh_m, sc_m)
        if is_attn:
            hc = modulate(rms_norm(ctx, g_mix_pre[i]), csh_m, csc_m)
            yx, yc = diff_attention(hx, hc, j, i, ang_row, ang_col, ctx_live,
                                    w_qkv, w_attn_out, lam_q1, lam_k1, lam_q2, lam_k2, g_subln)
        else:
            yx = chunk_mlp(hx, j, w_uv, b_uv, g_sgu, b_sgu, w_spatial, b_spatial, w_cmlp_out)
            if ctx_live:
                hc = modulate(rms_norm(ctx, g_mix_pre[i]), csh_m, csc_m)
                yc = chunk_mlp(hc, j, w_uv, b_uv, g_sgu, b_sgu, w_spatial, b_spatial, w_cmlp_out)
        x = x + g_m * rms_norm(yx, g_mix_post[i])
        fx = swiglu(modulate(rms_norm(x, g_ffn_pre[i]), sh_f, sc_f), i, w_ffn_in, w_ffn_out)
        x = x + g_f * rms_norm(fx, g_ffn_post[i])

        if ctx_live:
            ctx = ctx + cg_m * rms_norm(yc, g_mix_post[i])
            fc = swiglu(modulate(rms_norm(ctx, g_ffn_pre[i]), csh_f, csc_f), i, w_ffn_in, w_ffn_out)
            ctx = ctx + cg_f * rms_norm(fc, g_ffn_post[i])
    return x
```

```python
import functools
import math

import jax
import jax.numpy as jnp
from jax import lax
from jax.experimental import pallas as pl
from jax.experimental.pallas import tpu as pltpu

F32 = jnp.float32
BF16 = jnp.bfloat16

EPS = 1e-6
ROPE_GRID_W = 64
ROPE_BASE = 10000.0
N_MIXERS = 2
N_MOD = 6

LANES = 128
SUBLANES = 8
VMEM_LIMIT_BYTES = 56 << 20

ROW_BLOCK = 256
Q_BLOCK = 256
K_BLOCK = 512


def _const_spec(shape):
    return pl.BlockSpec(shape, lambda *_: (0,) * len(shape), pipeline_mode=pl.Buffered(1))


def _params(n_grid):
    return pltpu.CompilerParams(
        dimension_semantics=("arbitrary",) * n_grid, vmem_limit_bytes=VMEM_LIMIT_BYTES)


def _rms(x, g):
    return x * lax.rsqrt(jnp.mean(x * x, axis=-1, keepdims=True) + EPS) * g


def _modulated(x, g, shift, scale):
    return _rms(x, g) * (1.0 + scale) + shift


def _ada_kernel(c_ref, w_ref, b_ref, o_ref):
    s = jax.nn.silu(c_ref[...])
    o_ref[...] = jnp.dot(s, w_ref[...], preferred_element_type=F32,
                         precision=lax.Precision.HIGHEST) + b_ref[...]


def _ada_call(cond, w_ada, b_ada):
    depth, d, n = w_ada.shape
    tn = 2048
    return pl.pallas_call(
        _ada_kernel,
        grid=(depth, n // tn),
        in_specs=[
            pl.BlockSpec(cond.shape, lambda l, k: (0, 0)),
            pl.BlockSpec((None, d, tn), lambda l, k: (l, 0, k)),
            pl.BlockSpec((None, 1, tn), lambda l, k: (l, 0, k)),
        ],
        out_specs=pl.BlockSpec((None, cond.shape[0], tn), lambda l, k: (l, 0, k)),
        out_shape=jax.ShapeDtypeStruct((depth, cond.shape[0], n), F32),
        compiler_params=_params(2),
        name="ada_mod",
    )(cond, w_ada, b_ada.reshape(depth, 1, n))


def _residual_ffn(x, y, mod_ref, gpost_ref, gfpre_ref, gfpost_ref, win_ref, wout_ref):
    x = x + mod_ref[2:3, :] * _rms(y, gpost_ref[...])
    h = _modulated(x, gfpre_ref[...], mod_ref[3:4, :], mod_ref[4:5, :]).astype(BF16)
    gu = jnp.dot(h, win_ref[...], preferred_element_type=F32)
    f = wout_ref.shape[0]
    a = (jax.nn.silu(gu[:, :f]) * gu[:, f:]).astype(BF16)
    y2 = jnp.dot(a, wout_ref[...], preferred_element_type=F32)
    return x + mod_ref[5:6, :] * _rms(y2, gfpost_ref[...])


def _mod_spec(d, n_lat_blocks):
    return pl.BlockSpec((None, None, N_MOD, d),
                        lambda b, r: (b, jnp.minimum(r // n_lat_blocks, 1), 0, 0))


def _qkv_kernel(x_ref, mod_ref, g_ref, wqk_ref, wvt_ref, cos_ref, sa_ref, sb_ref,
                qk_ref, vt_ref, *, rot):
    h = _modulated(x_ref[...], g_ref[...], mod_ref[0:1, :], mod_ref[1:2, :]).astype(BF16)
    qk = jnp.dot(h, wqk_ref[...], preferred_element_type=F32)
    cos, sa, sb = cos_ref[...], sa_ref[...], sb_ref[...]
    for c in range(qk.shape[1] // LANES):
        blk = qk[:, c * LANES:(c + 1) * LANES]
        y = blk * cos + pltpu.roll(blk, LANES - rot, 1) * sa + pltpu.roll(blk, rot, 1) * sb
        qk_ref[c] = y.astype(BF16)
    vt = lax.dot_general(wvt_ref[...], h, (((1,), (1,)), ((), ())), preferred_element_type=F32)
    vt_ref[...] = vt.astype(BF16)


def _qkv_call(xt, mods, g_pre, wqk, wvt, cos, sa, sb, *, n_lat_blocks, rot):
    b, t, d = xt.shape
    tm = ROW_BLOCK
    n_qk = wqk.shape[1] // LANES
    return pl.pallas_call(
        functools.partial(_qkv_kernel, rot=rot),
        grid=(b, t // tm),
        in_specs=[
            pl.BlockSpec((None, tm, d), lambda bi, r: (bi, r, 0)),
            _mod_spec(d, n_lat_blocks),
            _const_spec((1, d)),
            _const_spec(wqk.shape),
            _const_spec(wvt.shape),
            pl.BlockSpec((tm, LANES), lambda bi, r: (r, 0)),
            pl.BlockSpec((tm, LANES), lambda bi, r: (r, 0)),
            pl.BlockSpec((tm, LANES), lambda bi, r: (r, 0)),
        ],
        out_specs=[
            pl.BlockSpec((None, n_qk, tm, LANES), lambda bi, r: (bi, 0, r, 0)),
            pl.BlockSpec((None, wvt.shape[0], tm), lambda bi, r: (bi, 0, r)),
        ],
        out_shape=[
            jax.ShapeDtypeStruct((b, n_qk, t, LANES), BF16),
            jax.ShapeDtypeStruct((b, wvt.shape[0], t), BF16),
        ],
        compiler_params=_params(2),
        name="qkv_rope",
    )(xt, mods, g_pre, wqk, wvt, cos, sa, sb)


def _attn_kernel(q_ref, k_ref, vt_ref, lam_ref, gsub_ref, o_ref, qst_ref, m_ref, l_ref, acc_ref,
                 *, dk, n_lat_keys, n_ctx_keys, n_lat_qblocks, lam_init):
    tq = q_ref.shape[0]
    qt = (q_ref[...].astype(F32) * (dk ** -0.5)).T
    row = lax.broadcasted_iota(jnp.int32, qt.shape, 0)
    qst_ref[:, :tq] = jnp.where(row < dk, qt, 0.0).astype(BF16)
    qst_ref[:, tq:] = jnp.where(row >= dk, qt, 0.0).astype(BF16)
    m_ref[...] = jnp.full(m_ref.shape, -jnp.inf, F32)
    l_ref[...] = jnp.zeros(l_ref.shape, F32)
    acc_ref[...] = jnp.zeros(acc_ref.shape, F32)

    def step(start, size):
        s = jnp.dot(k_ref[pl.ds(start, size), :], qst_ref[...],
                    preferred_element_type=F32)
        m_old = m_ref[...]
        m_new = jnp.maximum(m_old, jnp.max(s, axis=0, keepdims=True))
        alpha = jnp.exp(m_old - m_new)
        p = jnp.exp(s - m_new)
        l_ref[...] = alpha * l_ref[...] + jnp.sum(p, axis=0, keepdims=True)
        pv = jnp.dot(vt_ref[:, pl.ds(start, size)], p.astype(BF16),
                     preferred_element_type=F32)
        acc_ref[...] = alpha * acc_ref[...] + pv
        m_ref[...] = m_new

    @pl.when(pl.program_id(2) < n_lat_qblocks)
    def _():
        def body(j, carry):
            step(pl.multiple_of(j * K_BLOCK, K_BLOCK), K_BLOCK)
            return carry
        lax.fori_loop(0, n_lat_keys // K_BLOCK, body, 0)

    step(n_lat_keys, n_ctx_keys)

    lam = (jnp.exp(jnp.sum(lam_ref[0:1, :] * lam_ref[1:2, :], keepdims=True))
           - jnp.exp(jnp.sum(lam_ref[2:3, :] * lam_ref[3:4, :], keepdims=True)) + lam_init)
    acc = acc_ref[...]
    l = l_ref[...]
    o = acc[:, :tq] / l[:, :tq] - lam * (acc[:, tq:] / l[:, tq:])
    o_ref[...] = (_rms(o.T, gsub_ref[...]) * (1.0 - lam_init)).astype(BF16)


def _attn_call(qk, vt, lam_vecs, g_subln, *, n_heads, dk, n_lat_keys, n_ctx_keys, n_q_rows,
               lam_init):
    b, _, t, _ = qk.shape
    dv = vt.shape[1] // n_heads
    tq = Q_BLOCK
    kern = functools.partial(
        _attn_kernel, dk=dk, n_lat_keys=n_lat_keys, n_ctx_keys=n_ctx_keys,
        n_lat_qblocks=n_lat_keys // tq, lam_init=lam_init)
    return pl.pallas_call(
        kern,
        grid=(b, n_heads, n_q_rows // tq),
        in_specs=[
            pl.BlockSpec((None, None, tq, LANES), lambda bi, h, i: (bi, h, i, 0)),
            pl.BlockSpec((None, None, t, LANES), lambda bi, h, i: (bi, n_heads + h, 0, 0)),
            pl.BlockSpec((None, dv, t), lambda bi, h, i: (bi, h, 0)),
            pl.BlockSpec(lam_vecs.shape, lambda bi, h, i: (0, 0)),
            pl.BlockSpec((1, dv), lambda bi, h, i: (0, 0)),
        ],
        out_specs=pl.BlockSpec((None, tq, dv), lambda bi, h, i: (bi, i, h)),
        out_shape=jax.ShapeDtypeStruct((b, n_q_rows, n_heads * dv), BF16),
        scratch_shapes=[
            pltpu.VMEM((2 * dk, 2 * tq), BF16),
            pltpu.VMEM((1, 2 * tq), F32),
            pltpu.VMEM((1, 2 * tq), F32),
            pltpu.VMEM((dv, 2 * tq), F32),
        ],
        compiler_params=_params(3),
        name="diff_attn",
    )(qk, qk, vt, lam_vecs, g_subln)


def _attn_out_ffn_kernel(o_ref, x_ref, mod_ref, gpost_ref, gfpre_ref, gfpost_ref,
                         wo_ref, win_ref, wout_ref, out_ref):
    y = jnp.dot(o_ref[...], wo_ref[...], preferred_element_type=F32)
    out_ref[...] = _residual_ffn(x_ref[...], y, mod_ref, gpost_ref, gfpre_ref, gfpost_ref,
                                 win_ref, wout_ref)


def _attn_out_ffn_call(o, xt, mods, g_post, g_fpre, g_fpost, wo, win, wout, *, n_lat_blocks):
    b, rows, d = o.shape[0], o.shape[1], xt.shape[2]
    tm = ROW_BLOCK
    return pl.pallas_call(
        _attn_out_ffn_kernel,
        grid=(b, rows // tm),
        in_specs=[
            pl.BlockSpec((None, tm, o.shape[2]), lambda bi, r: (bi, r, 0)),
            pl.BlockSpec((None, tm, d), lambda bi, r: (bi, r, 0)),
            _mod_spec(d, n_lat_blocks),
            _const_spec((1, d)), _const_spec((1, d)), _const_spec((1, d)),
            _const_spec(wo.shape), _const_spec(win.shape), _const_spec(wout.shape),
        ],
        out_specs=pl.BlockSpec((None, tm, d), lambda bi, r: (bi, r, 0)),
        out_shape=jax.ShapeDtypeStruct((b, rows, d), F32),
        compiler_params=_params(2),
        name="attn_out_ffn",
    )(o, xt, mods, g_post, g_fpre, g_fpost, wo, win, wout)


def _cmlp_ffn_kernel(x_ref, mod_ref, gpre_ref, wuv_ref, buv_ref, gsgu_ref, bsgu_ref,
                     wsp_ref, bsp_ref, wco_ref, gpost_ref, gfpre_ref, gfpost_ref,
                     win_ref, wout_ref, out_ref):
    x = x_ref[...]
    h = _modulated(x, gpre_ref[...], mod_ref[0:1, :], mod_ref[1:2, :]).astype(BF16)
    z = jnp.dot(h, wuv_ref[...], preferred_element_type=F32) + buv_ref[...]
    z = 0.5 * z * (1.0 + lax.erf(z * math.sqrt(0.5)))
    w = z.shape[1] // 2
    u, v = z[:, :w], z[:, w:]
    mu = jnp.mean(v, axis=-1, keepdims=True)
    vc = v - mu
    var = jnp.mean(vc * vc, axis=-1, keepdims=True)
    vb = (vc * lax.rsqrt(var + EPS) * gsgu_ref[...] + bsgu_ref[...]).astype(BF16)

    groups, chunk, _ = wsp_ref.shape
    gd = w // groups
    n_chunks = x.shape[0] // chunk
    mixed = []
    for g in range(groups):
        rhs = jnp.concatenate(
            [vb[n * chunk:(n + 1) * chunk, g * gd:(g + 1) * gd] for n in range(n_chunks)], axis=1)
        mixed.append(jnp.dot(wsp_ref[g], rhs, preferred_element_type=F32) + bsp_ref[g])
    sv = jnp.concatenate(
        [jnp.concatenate([mixed[g][:, n * gd:(n + 1) * gd] for g in range(groups)], axis=1)
         for n in range(n_chunks)], axis=0)
    y = jnp.dot((u * sv).astype(BF16), wco_ref[...], preferred_element_type=F32)
    out_ref[...] = _residual_ffn(x, y, mod_ref, gpost_ref, gfpre_ref, gfpost_ref,
                                 win_ref, wout_ref)


def _cmlp_ffn_call(xt, mods, g_pre, wuv, buv, g_sgu, b_sgu, wsp, bsp, wco,
                   g_post, g_fpre, g_fpost, win, wout, *, n_lat_blocks):
    b, rows, d = xt.shape
    tm = ROW_BLOCK
    vec = lambda a: _const_spec((1, a.shape[-1]))
    return pl.pallas_call(
        _cmlp_ffn_kernel,
        grid=(b, rows // tm),
        in_specs=[
            pl.BlockSpec((None, tm, d), lambda bi, r: (bi, r, 0)),
            _mod_spec(d, n_lat_blocks),
            vec(g_pre), _const_spec(wuv.shape), vec(buv), vec(g_sgu), vec(b_sgu),
            _const_spec(wsp.shape), _const_spec(bsp.shape), _const_spec(wco.shape),
            vec(g_post), vec(g_fpre), vec(g_fpost),
            _const_spec(win.shape), _const_spec(wout.shape),
        ],
        out_specs=pl.BlockSpec((None, tm, d), lambda bi, r: (bi, r, 0)),
        out_shape=jax.ShapeDtypeStruct((b, rows, d), F32),
        compiler_params=_params(2),
        name="cmlp_ffn",
    )(xt, mods, g_pre, wuv, buv, g_sgu, b_sgu, wsp, bsp, wco, g_post, g_fpre, g_fpost, win, wout)


def _rope_tables(n_lat, n_ctx, dk):
    axis_dim = dk // 2
    rot = axis_dim // 2
    tok = jnp.arange(n_lat, dtype=jnp.int32)
    row = (tok // ROPE_GRID_W).astype(F32)
    col = (tok % ROPE_GRID_W).astype(F32)
    inv_freq = ROPE_BASE ** (-jnp.arange(0, axis_dim, 2, dtype=F32) / axis_dim)
    lane = jnp.arange(LANES, dtype=jnp.int32)
    in_map = lane % dk
    freq = inv_freq[in_map % rot]
    pos = jnp.where((in_map < axis_dim)[None, :], row[:, None], col[:, None])
    ang = pos * freq[None, :]
    first = ((in_map % axis_dim) < rot)[None, :]
    cos, sin = jnp.cos(ang), jnp.sin(ang)
    sa = jnp.where(first, -sin, 0.0)
    sb = jnp.where(first, 0.0, sin)
    pad = lambda a, v: jnp.concatenate([a, jnp.full((n_ctx, LANES), v, F32)], axis=0)
    return pad(cos, 1.0), pad(sa, 0.0), pad(sb, 0.0), rot


def kernel(x, c, ctx, c_ctx, w_ada, b_ada, g_mix_pre, g_mix_post, g_ffn_pre, g_ffn_post, w_qkv, w_attn_out, lam_q1, lam_k1, lam_q2, lam_k2, g_subln, w_uv, b_uv, g_sgu, b_sgu, w_spatial, b_spatial, w_cmlp_out, w_ffn_in, w_ffn_out):
    b, s, d = x.shape
    n_ctx = ctx.shape[1]
    depth = w_ada.shape[0]
    dk = lam_q1.shape[-1]
    dv = g_subln.shape[-1]
    n_heads = w_attn_out.shape[1] // dv
    qk_w = 2 * n_heads * dk * 2
    assert 2 * dk == LANES and dv == LANES
    assert s % K_BLOCK == 0 and s % ROW_BLOCK == 0 and n_ctx % ROW_BLOCK == 0
    assert ROW_BLOCK == Q_BLOCK and b + 1 <= SUBLANES

    cond = jnp.zeros((SUBLANES, d), F32).at[:b].set(c).at[b].set(c_ctx)
    mod_all = _ada_call(cond, w_ada, b_ada)
    mod_lat = mod_all[:, :b].reshape(depth, b, 1, N_MOD, d)
    mod_ctx = jnp.broadcast_to(mod_all[:, b].reshape(depth, 1, 1, N_MOD, d), mod_lat.shape)
    mods = jnp.concatenate([mod_lat, mod_ctx], axis=2)

    cos, sa, sb, rot = _rope_tables(s, n_ctx, dk)
    n_lat_blocks = s // ROW_BLOCK
    last_attn = max(i for i in range(depth) if i % N_MIXERS == 0)
    row_vec = lambda a: a.reshape(1, -1)

    xt = jnp.concatenate([x, ctx], axis=1)
    for i in range(depth):
        j = i // N_MIXERS
        ctx_live = i < last_attn
        win = w_ffn_in[i].astype(BF16)
        wout = w_ffn_out[i].astype(BF16)
        tail = (row_vec(g_mix_post[i]), row_vec(g_ffn_pre[i]), row_vec(g_ffn_post[i]))
        if i % N_MIXERS == 0:
            lam_init = 0.8 - 0.6 * math.exp(-0.3 * i)
            wqk = w_qkv[j][:, :qk_w].astype(BF16)
            wvt = w_qkv[j][:, qk_w:].T.astype(BF16)
            qk, vt = _qkv_call(xt, mods[i], row_vec(g_mix_pre[i]), wqk, wvt, cos, sa, sb,
                               n_lat_blocks=n_lat_blocks, rot=rot)
            lam_vecs = jnp.stack([lam_q1[j], lam_k1[j], lam_q2[j], lam_k2[j]]).astype(F32)
            o = _attn_call(qk, vt, lam_vecs, row_vec(g_subln[j]), n_heads=n_heads, dk=dk,
                           n_lat_keys=s, n_ctx_keys=n_ctx,
                           n_q_rows=s + n_ctx if ctx_live else s, lam_init=lam_init)
            xt = _attn_out_ffn_call(o, xt, mods[i], *tail, w_attn_out[j].astype(BF16), win, wout,
                                    n_lat_blocks=n_lat_blocks)
        else:
            if not ctx_live and xt.shape[1] != s:
                xt = xt[:, :s]
            xt = _cmlp_ffn_call(
                xt, mods[i], row_vec(g_mix_pre[i]), w_uv[j].astype(BF16), row_vec(b_uv[j]),
                row_vec(g_sgu[j]), row_vec(b_sgu[j]), w_spatial[j].astype(BF16),
                b_spatial[j][:, :, None], w_cmlp_out[j].astype(BF16), *tail, win, wout,
                n_lat_blocks=n_lat_blocks)
    return xt[:, :s] if xt.shape[1] != s else xt
```

```python
import functools
import math

import jax
import jax.numpy as jnp
from jax import lax
from jax.experimental import pallas as pl
from jax.experimental.pallas import tpu as pltpu

F32 = jnp.float32
BF16 = jnp.bfloat16

EPS = 1e-6
ROPE_GRID_W = 64
ROPE_BASE = 10000.0
N_MIXERS = 2
N_MOD = 6

LANES = 128
SUBLANES = 8
VMEM_LIMIT_BYTES = 56 << 20

ROW_BLOCK = 256
Q_BLOCK = 256
K_BLOCK = 512
K_UNROLL = 10


def _const_spec(shape):
    return pl.BlockSpec(shape, lambda *_: (0,) * len(shape), pipeline_mode=pl.Buffered(1))


def _params(n_grid):
    return pltpu.CompilerParams(
        dimension_semantics=("arbitrary",) * n_grid, vmem_limit_bytes=VMEM_LIMIT_BYTES)


def _rms(x, g):
    return x * lax.rsqrt(jnp.mean(x * x, axis=-1, keepdims=True) + EPS) * g


def _modulated(x, g, shift, scale):
    return _rms(x, g) * (1.0 + scale) + shift


def _ada_kernel(c_ref, w_ref, b_ref, o_ref):
    s = jax.nn.silu(c_ref[...])
    o_ref[...] = jnp.dot(s, w_ref[...], preferred_element_type=F32,
                         precision=lax.Precision.HIGHEST) + b_ref[...]


def _ada_call(cond, w_ada, b_ada):
    depth, d, n = w_ada.shape
    tn = 2048
    return pl.pallas_call(
        _ada_kernel,
        grid=(depth, n // tn),
        in_specs=[
            pl.BlockSpec(cond.shape, lambda l, k: (0, 0)),
            pl.BlockSpec((None, d, tn), lambda l, k: (l, 0, k)),
            pl.BlockSpec((None, 1, tn), lambda l, k: (l, 0, k)),
        ],
        out_specs=pl.BlockSpec((None, cond.shape[0], tn), lambda l, k: (l, 0, k)),
        out_shape=jax.ShapeDtypeStruct((depth, cond.shape[0], n), F32),
        compiler_params=_params(2),
        name="ada_mod",
    )(cond, w_ada, b_ada.reshape(depth, 1, n))


def _residual_ffn(x, y, mod_ref, gpost_ref, gfpre_ref, gfpost_ref, win_ref, wout_ref):
    x = x + mod_ref[2:3, :] * _rms(y, gpost_ref[...])
    h = _modulated(x, gfpre_ref[...], mod_ref[3:4, :], mod_ref[4:5, :]).astype(BF16)
    gu = jnp.dot(h, win_ref[...], preferred_element_type=F32)
    f = wout_ref.shape[0]
    a = (jax.nn.silu(gu[:, :f]) * gu[:, f:]).astype(BF16)
    y2 = jnp.dot(a, wout_ref[...], preferred_element_type=F32)
    return x + mod_ref[5:6, :] * _rms(y2, gfpost_ref[...])


def _mod_spec(d, n_lat_blocks):
    return pl.BlockSpec((None, None, N_MOD, d),
                        lambda b, r: (b, jnp.minimum(r // n_lat_blocks, 1), 0, 0))


def _qkv_kernel(x_ref, mod_ref, g_ref, wqk_ref, wvt_ref, cos_ref, sa_ref, sb_ref,
                qk_ref, vt_ref, *, rot, q_scale):
    h = _modulated(x_ref[...], g_ref[...], mod_ref[0:1, :], mod_ref[1:2, :]).astype(BF16)
    qk = jnp.dot(h, wqk_ref[...], preferred_element_type=F32)
    cos, sa, sb = cos_ref[...], sa_ref[...], sb_ref[...]
    n_blk = qk.shape[1] // LANES
    for c in range(n_blk):
        blk = qk[:, c * LANES:(c + 1) * LANES]
        y = blk * cos + pltpu.roll(blk, LANES - rot, 1) * sa + pltpu.roll(blk, rot, 1) * sb
        if c < n_blk // 2:
            y = y * q_scale
        qk_ref[c] = y.astype(BF16)
    vt = lax.dot_general(wvt_ref[...], h, (((1,), (1,)), ((), ())), preferred_element_type=F32)
    vt_ref[...] = vt.astype(BF16)


def _qkv_call(xt, mods, g_pre, wqk, wvt, cos, sa, sb, *, n_lat_blocks, rot, dk):
    b, t, d = xt.shape
    tm = ROW_BLOCK
    n_qk = wqk.shape[1] // LANES
    return pl.pallas_call(
        functools.partial(_qkv_kernel, rot=rot, q_scale=dk ** -0.5 * math.log2(math.e)),
        grid=(b, t // tm),
        in_specs=[
            pl.BlockSpec((None, tm, d), lambda bi, r: (bi, r, 0)),
            _mod_spec(d, n_lat_blocks),
            _const_spec((1, d)),
            _const_spec(wqk.shape),
            _const_spec(wvt.shape),
            pl.BlockSpec((tm, LANES), lambda bi, r: (r, 0)),
            pl.BlockSpec((tm, LANES), lambda bi, r: (r, 0)),
            pl.BlockSpec((tm, LANES), lambda bi, r: (r, 0)),
        ],
        out_specs=[
            pl.BlockSpec((None, n_qk, tm, LANES), lambda bi, r: (bi, 0, r, 0)),
            pl.BlockSpec((None, wvt.shape[0], tm), lambda bi, r: (bi, 0, r)),
        ],
        out_shape=[
            jax.ShapeDtypeStruct((b, n_qk, t, LANES), BF16),
            jax.ShapeDtypeStruct((b, wvt.shape[0], t), BF16),
        ],
        compiler_params=_params(2),
        name="qkv_rope",
    )(xt, mods, g_pre, wqk, wvt, cos, sa, sb)


def _attn_kernel(q_ref, k_ref, vt_ref, lam_ref, gsub_ref, o_ref, qst_ref, m_ref, l_ref, acc_ref,
                 *bufs, dk, n_lat_keys, n_ctx_keys, n_lat_qblocks, lam_init):
    tq = q_ref.shape[0]
    ctx_buf, lat_bufs = bufs[0:2], (bufs[2:4], bufs[4:6])

    qt = q_ref[...].astype(F32).T
    row = lax.broadcasted_iota(jnp.int32, qt.shape, 0)
    qst_ref[:, :tq] = jnp.where(row < dk, qt, 0.0).astype(BF16)
    qst_ref[:, tq:] = jnp.where(row >= dk, qt, 0.0).astype(BF16)
    m_ref[...] = jnp.full(m_ref.shape, -jnp.inf, F32)
    l_ref[...] = jnp.zeros(l_ref.shape, F32)
    acc_ref[...] = jnp.zeros(acc_ref.shape, F32)

    def scores(start, size, buf):
        s_ref, mx_ref = buf
        s = jnp.dot(k_ref[pl.ds(start, size), :], qst_ref[...],
                    preferred_element_type=F32)
        s_ref[...] = s
        mx_ref[...] = jnp.max(s, axis=0, keepdims=True)

    def absorb(start, size, buf):
        s_ref, mx_ref = buf
        m_old = m_ref[...]
        m_new = jnp.maximum(m_old, mx_ref[...])
        alpha = jnp.exp2(m_old - m_new)
        p = jnp.exp2(s_ref[...] - m_new)
        l_ref[...] = alpha * l_ref[...] + jnp.sum(p, axis=0, keepdims=True)
        pv = jnp.dot(vt_ref[:, pl.ds(start, size)], p.astype(BF16),
                     preferred_element_type=F32)
        acc_ref[...] = alpha * acc_ref[...] + pv
        m_ref[...] = m_new

    scores(n_lat_keys, n_ctx_keys, ctx_buf)

    @pl.when(pl.program_id(2) >= n_lat_qblocks)
    def _():
        absorb(n_lat_keys, n_ctx_keys, ctx_buf)

    @pl.when(pl.program_id(2) < n_lat_qblocks)
    def _():
        n_blk = n_lat_keys // K_BLOCK
        scores(0, K_BLOCK, lat_bufs[0])
        absorb(n_lat_keys, n_ctx_keys, ctx_buf)

        def body(j, carry):
            base = pl.multiple_of(j * (K_UNROLL * K_BLOCK), K_UNROLL * K_BLOCK)
            for u in range(K_UNROLL):
                scores(base + (u + 1) * K_BLOCK, K_BLOCK, lat_bufs[(u + 1) % 2])
                absorb(base + u * K_BLOCK, K_BLOCK, lat_bufs[u % 2])
            return carry
        n_groups = (n_blk - 1) // K_UNROLL
        lax.fori_loop(0, n_groups, body, 0)

        for blk in range(n_groups * K_UNROLL, n_blk):
            if blk + 1 < n_blk:
                scores((blk + 1) * K_BLOCK, K_BLOCK, lat_bufs[(blk + 1) % 2])
            absorb(blk * K_BLOCK, K_BLOCK, lat_bufs[blk % 2])

    lam = (jnp.exp(jnp.sum(lam_ref[0:1, :] * lam_ref[1:2, :], keepdims=True))
           - jnp.exp(jnp.sum(lam_ref[2:3, :] * lam_ref[3:4, :], keepdims=True)) + lam_init)
    acc = acc_ref[...]
    l = l_ref[...]
    o = acc[:, :tq] / l[:, :tq] - lam * (acc[:, tq:] / l[:, tq:])
    o_ref[...] = (_rms(o.T, gsub_ref[...]) * (1.0 - lam_init)).astype(BF16)


def _attn_call(qk, vt, lam_vecs, g_subln, *, n_heads, dk, n_lat_keys, n_ctx_keys, n_q_rows,
               lam_init):
    b, _, t, _ = qk.shape
    dv = vt.shape[1] // n_heads
    tq = Q_BLOCK
    kern = functools.partial(
        _attn_kernel, dk=dk, n_lat_keys=n_lat_keys, n_ctx_keys=n_ctx_keys,
        n_lat_qblocks=n_lat_keys // tq, lam_init=lam_init)

    def score_bufs(n_keys):
        return [pltpu.VMEM((n_keys, 2 * tq), F32), pltpu.VMEM((1, 2 * tq), F32)]

    return pl.pallas_call(
        kern,
        grid=(b, n_heads, n_q_rows // tq),
        in_specs=[
            pl.BlockSpec((None, None, tq, LANES), lambda bi, h, i: (bi, h, i, 0)),
            pl.BlockSpec((None, None, t, LANES), lambda bi, h, i: (bi, n_heads + h, 0, 0)),
            pl.BlockSpec((None, dv, t), lambda bi, h, i: (bi, h, 0)),
            pl.BlockSpec(lam_vecs.shape, lambda bi, h, i: (0, 0)),
            pl.BlockSpec((1, dv), lambda bi, h, i: (0, 0)),
        ],
        out_specs=pl.BlockSpec((None, tq, dv), lambda bi, h, i: (bi, i, h)),
        out_shape=jax.ShapeDtypeStruct((b, n_q_rows, n_heads * dv), BF16),
        scratch_shapes=[
            pltpu.VMEM((2 * dk, 2 * tq), BF16),
            pltpu.VMEM((1, 2 * tq), F32),
            pltpu.VMEM((1, 2 * tq), F32),
            pltpu.VMEM((dv, 2 * tq), F32),
            *score_bufs(n_ctx_keys), *score_bufs(K_BLOCK), *score_bufs(K_BLOCK),
        ],
        compiler_params=_params(3),
        name="diff_attn",
    )(qk, qk, vt, lam_vecs, g_subln)


def _attn_out_ffn_kernel(o_ref, x_ref, mod_ref, gpost_ref, gfpre_ref, gfpost_ref,
                         wo_ref, win_ref, wout_ref, out_ref):
    y = jnp.dot(o_ref[...], wo_ref[...], preferred_element_type=F32)
    out_ref[...] = _residual_ffn(x_ref[...], y, mod_ref, gpost_ref, gfpre_ref, gfpost_ref,
                                 win_ref, wout_ref)


def _attn_out_ffn_call(o, xt, mods, g_post, g_fpre, g_fpost, wo, win, wout, *, n_lat_blocks):
    b, rows, d = o.shape[0], o.shape[1], xt.shape[2]
    tm = ROW_BLOCK
    return pl.pallas_call(
        _attn_out_ffn_kernel,
        grid=(b, rows // tm),
        in_specs=[
            pl.BlockSpec((None, tm, o.shape[2]), lambda bi, r: (bi, r, 0)),
            pl.BlockSpec((None, tm, d), lambda bi, r: (bi, r, 0)),
            _mod_spec(d, n_lat_blocks),
            _const_spec((1, d)), _const_spec((1, d)), _const_spec((1, d)),
            _const_spec(wo.shape), _const_spec(win.shape), _const_spec(wout.shape),
        ],
        out_specs=pl.BlockSpec((None, tm, d), lambda bi, r: (bi, r, 0)),
        out_shape=jax.ShapeDtypeStruct((b, rows, d), F32),
        compiler_params=_params(2),
        name="attn_out_ffn",
    )(o, xt, mods, g_post, g_fpre, g_fpost, wo, win, wout)


def _cmlp_ffn_kernel(x_ref, mod_ref, gpre_ref, wuv_ref, buv_ref, gsgu_ref, bsgu_ref,
                     wsp_ref, bsp_ref, wco_ref, gpost_ref, gfpre_ref, gfpost_ref,
                     win_ref, wout_ref, out_ref):
    x = x_ref[...]
    h = _modulated(x, gpre_ref[...], mod_ref[0:1, :], mod_ref[1:2, :]).astype(BF16)
    z = jnp.dot(h, wuv_ref[...], preferred_element_type=F32) + buv_ref[...]
    z = 0.5 * z * (1.0 + lax.erf(z * math.sqrt(0.5)))
    w = z.shape[1] // 2
    u, v = z[:, :w], z[:, w:]
    mu = jnp.mean(v, axis=-1, keepdims=True)
    vc = v - mu
    var = jnp.mean(vc * vc, axis=-1, keepdims=True)
    vb = (vc * lax.rsqrt(var + EPS) * gsgu_ref[...] + bsgu_ref[...]).astype(BF16)

    groups, chunk, _ = wsp_ref.shape
    gd = w // groups
    n_chunks = x.shape[0] // chunk
    mixed = []
    for g in range(groups):
        rhs = jnp.concatenate(
            [vb[n * chunk:(n + 1) * chunk, g * gd:(g + 1) * gd] for n in range(n_chunks)], axis=1)
        mixed.append(jnp.dot(wsp_ref[g], rhs, preferred_element_type=F32) + bsp_ref[g])
    sv = jnp.concatenate(
        [jnp.concatenate([mixed[g][:, n * gd:(n + 1) * gd] for g in range(groups)], axis=1)
         for n in range(n_chunks)], axis=0)
    y = jnp.dot((u * sv).astype(BF16), wco_ref[...], preferred_element_type=F32)
    out_ref[...] = _residual_ffn(x, y, mod_ref, gpost_ref, gfpre_ref, gfpost_ref,
                                 win_ref, wout_ref)


def _cmlp_ffn_call(xt, mods, g_pre, wuv, buv, g_sgu, b_sgu, wsp, bsp, wco,
                   g_post, g_fpre, g_fpost, win, wout, *, n_lat_blocks):
    b, rows, d = xt.shape
    tm = ROW_BLOCK
    vec = lambda a: _const_spec((1, a.shape[-1]))
    return pl.pallas_call(
        _cmlp_ffn_kernel,
        grid=(b, rows // tm),
        in_specs=[
            pl.BlockSpec((None, tm, d), lambda bi, r: (bi, r, 0)),
            _mod_spec(d, n_lat_blocks),
            vec(g_pre), _const_spec(wuv.shape), vec(buv), vec(g_sgu), vec(b_sgu),
            _const_spec(wsp.shape), _const_spec(bsp.shape), _const_spec(wco.shape),
            vec(g_post), vec(g_fpre), vec(g_fpost),
            _const_spec(win.shape), _const_spec(wout.shape),
        ],
        out_specs=pl.BlockSpec((None, tm, d), lambda bi, r: (bi, r, 0)),
        out_shape=jax.ShapeDtypeStruct((b, rows, d), F32),
        compiler_params=_params(2),
        name="cmlp_ffn",
    )(xt, mods, g_pre, wuv, buv, g_sgu, b_sgu, wsp, bsp, wco, g_post, g_fpre, g_fpost, win, wout)


def _rope_tables(n_lat, n_ctx, dk):
    axis_dim = dk // 2
    rot = axis_dim // 2
    tok = jnp.arange(n_lat, dtype=jnp.int32)
    row = (tok // ROPE_GRID_W).astype(F32)
    col = (tok % ROPE_GRID_W).astype(F32)
    inv_freq = ROPE_BASE ** (-jnp.arange(0, axis_dim, 2, dtype=F32) / axis_dim)
    lane = jnp.arange(LANES, dtype=jnp.int32)
    in_map = lane % dk
    freq = inv_freq[in_map % rot]
    pos = jnp.where((in_map < axis_dim)[None, :], row[:, None], col[:, None])
    ang = pos * freq[None, :]
    first = ((in_map % axis_dim) < rot)[None, :]
    cos, sin = jnp.cos(ang), jnp.sin(ang)
    sa = jnp.where(first, -sin, 0.0)
    sb = jnp.where(first, 0.0, sin)
    pad = lambda a, v: jnp.concatenate([a, jnp.full((n_ctx, LANES), v, F32)], axis=0)
    return pad(cos, 1.0), pad(sa, 0.0), pad(sb, 0.0), rot


def kernel(x, c, ctx, c_ctx, w_ada, b_ada, g_mix_pre, g_mix_post, g_ffn_pre, g_ffn_post, w_qkv, w_attn_out, lam_q1, lam_k1, lam_q2, lam_k2, g_subln, w_uv, b_uv, g_sgu, b_sgu, w_spatial, b_spatial, w_cmlp_out, w_ffn_in, w_ffn_out):
    b, s, d = x.shape
    n_ctx = ctx.shape[1]
    depth = w_ada.shape[0]
    dk = lam_q1.shape[-1]
    dv = g_subln.shape[-1]
    n_heads = w_attn_out.shape[1] // dv
    qk_w = 2 * n_heads * dk * 2
    assert 2 * dk == LANES and dv == LANES
    assert K_UNROLL % 2 == 0 and s % K_BLOCK == 0 and s % ROW_BLOCK == 0 and n_ctx % ROW_BLOCK == 0
    assert ROW_BLOCK == Q_BLOCK and b + 1 <= SUBLANES

    cond = jnp.zeros((SUBLANES, d), F32).at[:b].set(c).at[b].set(c_ctx)
    mod_all = _ada_call(cond, w_ada, b_ada)
    mod_lat = mod_all[:, :b].reshape(depth, b, 1, N_MOD, d)
    mod_ctx = jnp.broadcast_to(mod_all[:, b].reshape(depth, 1, 1, N_MOD, d), mod_lat.shape)
    mods = jnp.concatenate([mod_lat, mod_ctx], axis=2)

    cos, sa, sb, rot = _rope_tables(s, n_ctx, dk)
    n_lat_blocks = s // ROW_BLOCK
    last_attn = max(i for i in range(depth) if i % N_MIXERS == 0)
    row_vec = lambda a: a.reshape(1, -1)

    xt = jnp.concatenate([x, ctx], axis=1)
    for i in range(depth):
        j = i // N_MIXERS
        ctx_live = i < last_attn
        win = w_ffn_in[i].astype(BF16)
        wout = w_ffn_out[i].astype(BF16)
        tail = (row_vec(g_mix_post[i]), row_vec(g_ffn_pre[i]), row_vec(g_ffn_post[i]))
        if i % N_MIXERS == 0:
            lam_init = 0.8 - 0.6 * math.exp(-0.3 * i)
            wqk = w_qkv[j][:, :qk_w].astype(BF16)
            wvt = w_qkv[j][:, qk_w:].T.astype(BF16)
            qk, vt = _qkv_call(xt, mods[i], row_vec(g_mix_pre[i]), wqk, wvt, cos, sa, sb,
                               n_lat_blocks=n_lat_blocks, rot=rot, dk=dk)
            lam_vecs = jnp.stack([lam_q1[j], lam_k1[j], lam_q2[j], lam_k2[j]]).astype(F32)
            o = _attn_call(qk, vt, lam_vecs, row_vec(g_subln[j]), n_heads=n_heads, dk=dk,
                           n_lat_keys=s, n_ctx_keys=n_ctx,
                           n_q_rows=s + n_ctx if ctx_live else s, lam_init=lam_init)
            xt = _attn_out_ffn_call(o, xt, mods[i], *tail, w_attn_out[j].astype(BF16), win, wout,
                                    n_lat_blocks=n_lat_blocks)
        else:
            if not ctx_live and xt.shape[1] != s:
                xt = xt[:, :s]
            xt = _cmlp_ffn_call(
                xt, mods[i], row_vec(g_mix_pre[i]), w_uv[j].astype(BF16), row_vec(b_uv[j]),
                row_vec(g_sgu[j]), row_vec(b_sgu[j]), w_spatial[j].astype(BF16),
                b_spatial[j][:, :, None], w_cmlp_out[j].astype(BF16), *tail, win, wout,
                n_lat_blocks=n_lat_blocks)
    return xt[:, :s] if xt.shape[1] != s else xt
```

```python
import functools
import math

import jax
import jax.numpy as jnp
from jax import lax
from jax.experimental import pallas as pl
from jax.experimental.pallas import tpu as pltpu

F32 = jnp.float32
BF16 = jnp.bfloat16

EPS = 1e-6
ROPE_GRID_W = 64
ROPE_BASE = 10000.0
N_MIXERS = 2
N_MOD = 6

LANES = 128
SUBLANES = 8
VMEM_LIMIT_BYTES = 56 << 20

ROW_BLOCK = 256
Q_BLOCK = 256
K_BLOCK = 1024
V_PAD_ROWS = 16
K_RING = 3
K_UNROLL = 6


def _const_spec(shape):
    return pl.BlockSpec(shape, lambda *_: (0,) * len(shape), pipeline_mode=pl.Buffered(1))


def _params(n_grid, flags=None):
    return pltpu.CompilerParams(
        dimension_semantics=("arbitrary",) * n_grid, vmem_limit_bytes=VMEM_LIMIT_BYTES,
        flags=flags)


def _rms(x, g):
    return x * lax.rsqrt(jnp.mean(x * x, axis=-1, keepdims=True) + EPS) * g


def _modulated(x, g, shift, scale):
    return _rms(x, g) * (1.0 + scale) + shift


def _ada_kernel(c_ref, w_ref, b_ref, o_ref):
    s = jax.nn.silu(c_ref[...])
    o_ref[...] = jnp.dot(s, w_ref[...], preferred_element_type=F32,
                         precision=lax.Precision.HIGHEST) + b_ref[...]


def _ada_call(cond, w_ada, b_ada):
    depth, d, n = w_ada.shape
    tn = 2048
    return pl.pallas_call(
        _ada_kernel,
        grid=(depth, n // tn),
        in_specs=[
            pl.BlockSpec(cond.shape, lambda l, k: (0, 0)),
            pl.BlockSpec((None, d, tn), lambda l, k: (l, 0, k)),
            pl.BlockSpec((None, 1, tn), lambda l, k: (l, 0, k)),
        ],
        out_specs=pl.BlockSpec((None, cond.shape[0], tn), lambda l, k: (l, 0, k)),
        out_shape=jax.ShapeDtypeStruct((depth, cond.shape[0], n), F32),
        compiler_params=_params(2),
        name="ada_mod",
    )(cond, w_ada, b_ada.reshape(depth, 1, n))


def _residual_ffn(x, y, mod_ref, gpost_ref, gfpre_ref, gfpost_ref, win_ref, wout_ref):
    x = x + mod_ref[2:3, :] * _rms(y, gpost_ref[...])
    h = _modulated(x, gfpre_ref[...], mod_ref[3:4, :], mod_ref[4:5, :]).astype(BF16)
    gu = jnp.dot(h, win_ref[...], preferred_element_type=F32)
    f = wout_ref.shape[0]
    a = (jax.nn.silu(gu[:, :f]) * gu[:, f:]).astype(BF16)
    y2 = jnp.dot(a, wout_ref[...], preferred_element_type=F32)
    return x + mod_ref[5:6, :] * _rms(y2, gfpost_ref[...])


def _mod_spec(d, n_lat_blocks):
    return pl.BlockSpec((None, None, N_MOD, d),
                        lambda b, r: (b, jnp.minimum(r // n_lat_blocks, 1), 0, 0))


def _qkv_kernel(x_ref, mod_ref, g_ref, wqk_ref, wvt_ref, cos_ref, sa_ref, sb_ref,
                qk_ref, vt_ref, *, rot, q_scale):
    h = _modulated(x_ref[...], g_ref[...], mod_ref[0:1, :], mod_ref[1:2, :]).astype(BF16)
    qk = jnp.dot(h, wqk_ref[...], preferred_element_type=F32)
    cos, sa, sb = cos_ref[...], sa_ref[...], sb_ref[...]
    n_blk = qk.shape[1] // LANES
    for c in range(n_blk):
        blk = qk[:, c * LANES:(c + 1) * LANES]
        y = blk * cos + pltpu.roll(blk, LANES - rot, 1) * sa + pltpu.roll(blk, rot, 1) * sb
        if c < n_blk // 2:
            y = y * q_scale
        qk_ref[c] = y.astype(BF16)
    vt = lax.dot_general(wvt_ref[...], h, (((1,), (1,)), ((), ())), preferred_element_type=F32)
    n_heads, dv_ext, tm = vt_ref.shape
    dv = vt.shape[0] // n_heads
    pad_row = lax.broadcasted_iota(jnp.int32, (dv_ext - dv, tm), 0)
    pad = jnp.where(pad_row == 0, 1.0, 0.0).astype(BF16)
    for hd in range(n_heads):
        vt_ref[hd, :dv, :] = vt[hd * dv:(hd + 1) * dv, :].astype(BF16)
        vt_ref[hd, dv:, :] = pad


def _qkv_call(xt, mods, g_pre, wqk, wvt, cos, sa, sb, *, n_lat_blocks, rot, dk, n_heads):
    b, t, d = xt.shape
    tm = ROW_BLOCK
    n_qk = wqk.shape[1] // LANES
    dv_ext = wvt.shape[0] // n_heads + V_PAD_ROWS
    return pl.pallas_call(
        functools.partial(_qkv_kernel, rot=rot, q_scale=dk ** -0.5 * math.log2(math.e)),
        grid=(b, t // tm),
        in_specs=[
            pl.BlockSpec((None, tm, d), lambda bi, r: (bi, r, 0)),
            _mod_spec(d, n_lat_blocks),
            _const_spec((1, d)),
            _const_spec(wqk.shape),
            _const_spec(wvt.shape),
            pl.BlockSpec((tm, LANES), lambda bi, r: (r, 0)),
            pl.BlockSpec((tm, LANES), lambda bi, r: (r, 0)),
            pl.BlockSpec((tm, LANES), lambda bi, r: (r, 0)),
        ],
        out_specs=[
            pl.BlockSpec((None, n_qk, tm, LANES), lambda bi, r: (bi, 0, r, 0)),
            pl.BlockSpec((None, n_heads, dv_ext, tm), lambda bi, r: (bi, 0, 0, r)),
        ],
        out_shape=[
            jax.ShapeDtypeStruct((b, n_qk, t, LANES), BF16),
            jax.ShapeDtypeStruct((b, n_heads, dv_ext, t), BF16),
        ],
        compiler_params=_params(2),
        name="qkv_rope",
    )(xt, mods, g_pre, wqk, wvt, cos, sa, sb)


def _attn_kernel(q_ref, k_ref, vt_ref, lam_ref, gsub_ref, o_ref, qst_ref, m_ref, acc_ref,
                 *bufs, dk, n_lat_keys, n_ctx_keys, n_lat_qblocks, lam_init):
    tq = q_ref.shape[0]
    ctx_buf = bufs[0:2]
    lat_bufs = tuple(bufs[i:i + 2] for i in range(2, len(bufs), 2))

    qt = q_ref[...].astype(F32).T
    row = lax.broadcasted_iota(jnp.int32, qt.shape, 0)
    qst_ref[:, :tq] = jnp.where(row < dk, qt, 0.0).astype(BF16)
    qst_ref[:, tq:] = jnp.where(row >= dk, qt, 0.0).astype(BF16)
    m_ref[...] = jnp.full(m_ref.shape, -jnp.inf, F32)
    acc_ref[...] = jnp.zeros(acc_ref.shape, F32)

    def scores(start, size, buf):
        s_ref, mx_ref = buf
        s = jnp.dot(k_ref[pl.ds(start, size), :], qst_ref[...],
                    preferred_element_type=F32)
        s_ref[...] = s
        mx_ref[...] = jnp.max(s, axis=0, keepdims=True)

    def absorb(start, size, buf):
        s_ref, mx_ref = buf
        m_old = m_ref[...]
        m_new = jnp.maximum(m_old, mx_ref[...])
        alpha = jnp.exp2(m_old - m_new)
        p = jnp.exp2(s_ref[...] - m_new)
        pv = jnp.dot(vt_ref[:, pl.ds(start, size)], p.astype(BF16),
                     preferred_element_type=F32)
        acc_ref[...] = alpha * acc_ref[...] + pv
        m_ref[...] = m_new

    scores(n_lat_keys, n_ctx_keys, ctx_buf)

    @pl.when(pl.program_id(2) >= n_lat_qblocks)
    def _():
        absorb(n_lat_keys, n_ctx_keys, ctx_buf)

    @pl.when(pl.program_id(2) < n_lat_qblocks)
    def _():
        n_blk = n_lat_keys // K_BLOCK
        ahead = len(lat_bufs) - 1
        for blk in range(ahead):
            scores(blk * K_BLOCK, K_BLOCK, lat_bufs[blk])
        absorb(n_lat_keys, n_ctx_keys, ctx_buf)

        def body(j, carry):
            base = pl.multiple_of(j * (K_UNROLL * K_BLOCK), K_UNROLL * K_BLOCK)
            for u in range(K_UNROLL):
                scores(base + (u + ahead) * K_BLOCK, K_BLOCK, lat_bufs[(u + ahead) % len(lat_bufs)])
                absorb(base + u * K_BLOCK, K_BLOCK, lat_bufs[u % len(lat_bufs)])
            return carry
        n_groups = (n_blk - ahead) // K_UNROLL
        lax.fori_loop(0, n_groups, body, 0)

        for blk in range(n_groups * K_UNROLL, n_blk):
            if blk + ahead < n_blk:
                scores((blk + ahead) * K_BLOCK, K_BLOCK, lat_bufs[(blk + ahead) % len(lat_bufs)])
            absorb(blk * K_BLOCK, K_BLOCK, lat_bufs[blk % len(lat_bufs)])

    lam = (jnp.exp(jnp.sum(lam_ref[0:1, :] * lam_ref[1:2, :], keepdims=True))
           - jnp.exp(jnp.sum(lam_ref[2:3, :] * lam_ref[3:4, :], keepdims=True)) + lam_init)
    dv = o_ref.shape[1]
    acc = acc_ref[:dv, :]
    l = acc_ref[dv:dv + 1, :]
    o = acc[:, :tq] / l[:, :tq] - lam * (acc[:, tq:] / l[:, tq:])
    o_ref[...] = (_rms(o.T, gsub_ref[...]) * (1.0 - lam_init)).astype(BF16)


def _attn_call(qk, vt, lam_vecs, g_subln, *, n_heads, dk, n_lat_keys, n_ctx_keys, n_q_rows,
               lam_init):
    b, _, t, _ = qk.shape
    dv_ext = vt.shape[2]
    dv = dv_ext - V_PAD_ROWS
    tq = Q_BLOCK
    kern = functools.partial(
        _attn_kernel, dk=dk, n_lat_keys=n_lat_keys, n_ctx_keys=n_ctx_keys,
        n_lat_qblocks=n_lat_keys // tq, lam_init=lam_init)

    def score_bufs(n_keys):
        return [pltpu.VMEM((n_keys, 2 * tq), F32), pltpu.VMEM((1, 2 * tq), F32)]

    return pl.pallas_call(
        kern,
        grid=(b, n_heads, n_q_rows // tq),
        in_specs=[
            pl.BlockSpec((None, None, tq, LANES), lambda bi, h, i: (bi, h, i, 0)),
            pl.BlockSpec((None, None, t, LANES), lambda bi, h, i: (bi, n_heads + h, 0, 0)),
            pl.BlockSpec((None, None, dv_ext, t), lambda bi, h, i: (bi, h, 0, 0)),
            pl.BlockSpec(lam_vecs.shape, lambda bi, h, i: (0, 0)),
            pl.BlockSpec((1, dv), lambda bi, h, i: (0, 0)),
        ],
        out_specs=pl.BlockSpec((None, tq, dv), lambda bi, h, i: (bi, i, h)),
        out_shape=jax.ShapeDtypeStruct((b, n_q_rows, n_heads * dv), BF16),
        scratch_shapes=[
            pltpu.VMEM((2 * dk, 2 * tq), BF16),
            pltpu.VMEM((1, 2 * tq), F32),
            pltpu.VMEM((dv_ext, 2 * tq), F32),
            *score_bufs(n_ctx_keys), *(K_RING * score_bufs(K_BLOCK)),
        ],
        compiler_params=_params(3),
        name="diff_attn",
    )(qk, qk, vt, lam_vecs, g_subln)


def _attn_out_ffn_kernel(o_ref, x_ref, mod_ref, gpost_ref, gfpre_ref, gfpost_ref,
                         wo_ref, win_ref, wout_ref, out_ref):
    y = jnp.dot(o_ref[...], wo_ref[...], preferred_element_type=F32)
    out_ref[...] = _residual_ffn(x_ref[...], y, mod_ref, gpost_ref, gfpre_ref, gfpost_ref,
                                 win_ref, wout_ref)


def _attn_out_ffn_call(o, xt, mods, g_post, g_fpre, g_fpost, wo, win, wout, *, n_lat_blocks):
    b, rows, d = o.shape[0], o.shape[1], xt.shape[2]
    tm = ROW_BLOCK
    return pl.pallas_call(
        _attn_out_ffn_kernel,
        grid=(b, rows // tm),
        in_specs=[
            pl.BlockSpec((None, tm, o.shape[2]), lambda bi, r: (bi, r, 0)),
            pl.BlockSpec((None, tm, d), lambda bi, r: (bi, r, 0)),
            _mod_spec(d, n_lat_blocks),
            _const_spec((1, d)), _const_spec((1, d)), _const_spec((1, d)),
            _const_spec(wo.shape), _const_spec(win.shape), _const_spec(wout.shape),
        ],
        out_specs=pl.BlockSpec((None, tm, d), lambda bi, r: (bi, r, 0)),
        out_shape=jax.ShapeDtypeStruct((b, rows, d), F32),
        compiler_params=_params(2),
        name="attn_out_ffn",
    )(o, xt, mods, g_post, g_fpre, g_fpost, wo, win, wout)


def _cmlp_ffn_kernel(x_ref, mod_ref, gpre_ref, wuv_ref, buv_ref, gsgu_ref, bsgu_ref,
                     wsp_ref, bsp_ref, wco_ref, gpost_ref, gfpre_ref, gfpost_ref,
                     win_ref, wout_ref, out_ref):
    x = x_ref[...]
    h = _modulated(x, gpre_ref[...], mod_ref[0:1, :], mod_ref[1:2, :]).astype(BF16)
    z = jnp.dot(h, wuv_ref[...], preferred_element_type=F32) + buv_ref[...]
    z = 0.5 * z * (1.0 + lax.erf(z * math.sqrt(0.5)))
    w = z.shape[1] // 2
    u, v = z[:, :w], z[:, w:]
    mu = jnp.mean(v, axis=-1, keepdims=True)
    vc = v - mu
    var = jnp.mean(vc * vc, axis=-1, keepdims=True)
    vb = (vc * lax.rsqrt(var + EPS) * gsgu_ref[...] + bsgu_ref[...]).astype(BF16)

    groups, chunk, _ = wsp_ref.shape
    gd = w // groups
    n_chunks = x.shape[0] // chunk
    mixed = []
    for g in range(groups):
        rhs = jnp.concatenate(
            [vb[n * chunk:(n + 1) * chunk, g * gd:(g + 1) * gd] for n in range(n_chunks)], axis=1)
        mixed.append(jnp.dot(wsp_ref[g], rhs, preferred_element_type=F32) + bsp_ref[g])
    sv = jnp.concatenate(
        [jnp.concatenate([mixed[g][:, n * gd:(n + 1) * gd] for g in range(groups)], axis=1)
         for n in range(n_chunks)], axis=0)
    y = jnp.dot((u * sv).astype(BF16), wco_ref[...], preferred_element_type=F32)
    out_ref[...] = _residual_ffn(x, y, mod_ref, gpost_ref, gfpre_ref, gfpost_ref,
                                 win_ref, wout_ref)


def _cmlp_ffn_call(xt, mods, g_pre, wuv, buv, g_sgu, b_sgu, wsp, bsp, wco,
                   g_post, g_fpre, g_fpost, win, wout, *, n_lat_blocks):
    b, rows, d = xt.shape
    tm = ROW_BLOCK
    vec = lambda a: _const_spec((1, a.shape[-1]))
    return pl.pallas_call(
        _cmlp_ffn_kernel,
        grid=(b, rows // tm),
        in_specs=[
            pl.BlockSpec((None, tm, d), lambda bi, r: (bi, r, 0)),
            _mod_spec(d, n_lat_blocks),
            vec(g_pre), _const_spec(wuv.shape), vec(buv), vec(g_sgu), vec(b_sgu),
            _const_spec(wsp.shape), _const_spec(bsp.shape), _const_spec(wco.shape),
            vec(g_post), vec(g_fpre), vec(g_fpost),
            _const_spec(win.shape), _const_spec(wout.shape),
        ],
        out_specs=pl.BlockSpec((None, tm, d), lambda bi, r: (bi, r, 0)),
        out_shape=jax.ShapeDtypeStruct((b, rows, d), F32),
        compiler_params=_params(2),
        name="cmlp_ffn",
    )(xt, mods, g_pre, wuv, buv, g_sgu, b_sgu, wsp, bsp, wco, g_post, g_fpre, g_fpost, win, wout)


def _rope_tables(n_lat, n_ctx, dk):
    axis_dim = dk // 2
    rot = axis_dim // 2
    tok = jnp.arange(n_lat, dtype=jnp.int32)
    row = (tok // ROPE_GRID_W).astype(F32)
    col = (tok % ROPE_GRID_W).astype(F32)
    inv_freq = ROPE_BASE ** (-jnp.arange(0, axis_dim, 2, dtype=F32) / axis_dim)
    lane = jnp.arange(LANES, dtype=jnp.int32)
    in_map = lane % dk
    freq = inv_freq[in_map % rot]
    pos = jnp.where((in_map < axis_dim)[None, :], row[:, None], col[:, None])
    ang = pos * freq[None, :]
    first = ((in_map % axis_dim) < rot)[None, :]
    cos, sin = jnp.cos(ang), jnp.sin(ang)
    sa = jnp.where(first, -sin, 0.0)
    sb = jnp.where(first, 0.0, sin)
    pad = lambda a, v: jnp.concatenate([a, jnp.full((n_ctx, LANES), v, F32)], axis=0)
    return pad(cos, 1.0), pad(sa, 0.0), pad(sb, 0.0), rot


def kernel(x, c, ctx, c_ctx, w_ada, b_ada, g_mix_pre, g_mix_post, g_ffn_pre, g_ffn_post, w_qkv, w_attn_out, lam_q1, lam_k1, lam_q2, lam_k2, g_subln, w_uv, b_uv, g_sgu, b_sgu, w_spatial, b_spatial, w_cmlp_out, w_ffn_in, w_ffn_out):
    b, s, d = x.shape
    n_ctx = ctx.shape[1]
    depth = w_ada.shape[0]
    dk = lam_q1.shape[-1]
    dv = g_subln.shape[-1]
    n_heads = w_attn_out.shape[1] // dv
    qk_w = 2 * n_heads * dk * 2
    assert 2 * dk == LANES and dv == LANES
    assert K_UNROLL % K_RING == 0 and s % K_BLOCK == 0 and s >= (K_RING - 1) * K_BLOCK and s % ROW_BLOCK == 0 and n_ctx % ROW_BLOCK == 0
    assert ROW_BLOCK == Q_BLOCK and b + 1 <= SUBLANES

    cond = jnp.zeros((SUBLANES, d), F32).at[:b].set(c).at[b].set(c_ctx)
    mod_all = _ada_call(cond, w_ada, b_ada)
    mod_lat = mod_all[:, :b].reshape(depth, b, 1, N_MOD, d)
    mod_ctx = jnp.broadcast_to(mod_all[:, b].reshape(depth, 1, 1, N_MOD, d), mod_lat.shape)
    mods = jnp.concatenate([mod_lat, mod_ctx], axis=2)

    cos, sa, sb, rot = _rope_tables(s, n_ctx, dk)
    n_lat_blocks = s // ROW_BLOCK
    last_attn = max(i for i in range(depth) if i % N_MIXERS == 0)
    row_vec = lambda a: a.reshape(1, -1)

    xt = jnp.concatenate([x, ctx], axis=1)
    for i in range(depth):
        j = i // N_MIXERS
        ctx_live = i < last_attn
        win = w_ffn_in[i].astype(BF16)
        wout = w_ffn_out[i].astype(BF16)
        tail = (row_vec(g_mix_post[i]), row_vec(g_ffn_pre[i]), row_vec(g_ffn_post[i]))
        if i % N_MIXERS == 0:
            lam_init = 0.8 - 0.6 * math.exp(-0.3 * i)
            wqk = w_qkv[j][:, :qk_w].astype(BF16)
            wvt = w_qkv[j][:, qk_w:].T.astype(BF16)
            qk, vt = _qkv_call(xt, mods[i], row_vec(g_mix_pre[i]), wqk, wvt, cos, sa, sb,
                               n_lat_blocks=n_lat_blocks, rot=rot, dk=dk, n_heads=n_heads)
            lam_vecs = jnp.stack([lam_q1[j], lam_k1[j], lam_q2[j], lam_k2[j]]).astype(F32)
            o = _attn_call(qk, vt, lam_vecs, row_vec(g_subln[j]), n_heads=n_heads, dk=dk,
                           n_lat_keys=s, n_ctx_keys=n_ctx,
                           n_q_rows=s + n_ctx if ctx_live else s, lam_init=lam_init)
            xt = _attn_out_ffn_call(o, xt, mods[i], *tail, w_attn_out[j].astype(BF16), win, wout,
                                    n_lat_blocks=n_lat_blocks)
        else:
            if not ctx_live and xt.shape[1] != s:
                xt = xt[:, :s]
            xt = _cmlp_ffn_call(
                xt, mods[i], row_vec(g_mix_pre[i]), w_uv[j].astype(BF16), row_vec(b_uv[j]),
                row_vec(g_sgu[j]), row_vec(b_sgu[j]), w_spatial[j].astype(BF16),
                b_spatial[j][:, :, None], w_cmlp_out[j].astype(BF16), *tail, win, wout,
                n_lat_blocks=n_lat_blocks)
    return xt[:, :s] if xt.shape[1] != s else xt
```

```python
import functools
import math

import jax
import jax.numpy as jnp
from jax import lax
from jax.experimental import pallas as pl
from jax.experimental.pallas import tpu as pltpu

F32 = jnp.float32
BF16 = jnp.bfloat16

EPS = 1e-6
ROPE_GRID_W = 64
ROPE_BASE = 10000.0
N_MIXERS = 2
N_MOD = 6

LANES = 128
SUBLANES = 8
VMEM_LIMIT_BYTES = 56 << 20

ROW_BLOCK = 256
Q_BLOCK = 256
Q_PER_STEP = 2
K_BLOCK = 1024
V_PAD_ROWS = 16
K_RING = 3
K_UNROLL = 6


def _const_spec(shape):
    return pl.BlockSpec(shape, lambda *_: (0,) * len(shape), pipeline_mode=pl.Buffered(1))


def _params(n_grid, flags=None):
    return pltpu.CompilerParams(
        dimension_semantics=("arbitrary",) * n_grid, vmem_limit_bytes=VMEM_LIMIT_BYTES,
        flags=flags)


def _rms(x, g):
    return x * lax.rsqrt(jnp.mean(x * x, axis=-1, keepdims=True) + EPS) * g


def _modulated(x, g, shift, scale):
    return _rms(x, g) * (1.0 + scale) + shift


def _ada_kernel(c_ref, w_ref, b_ref, o_ref):
    s = jax.nn.silu(c_ref[...])
    o_ref[...] = jnp.dot(s, w_ref[...], preferred_element_type=F32,
                         precision=lax.Precision.HIGHEST) + b_ref[...]


def _ada_call(cond, w_ada, b_ada):
    depth, d, n = w_ada.shape
    tn = 2048
    return pl.pallas_call(
        _ada_kernel,
        grid=(depth, n // tn),
        in_specs=[
            pl.BlockSpec(cond.shape, lambda l, k: (0, 0)),
            pl.BlockSpec((None, d, tn), lambda l, k: (l, 0, k)),
            pl.BlockSpec((None, 1, tn), lambda l, k: (l, 0, k)),
        ],
        out_specs=pl.BlockSpec((None, cond.shape[0], tn), lambda l, k: (l, 0, k)),
        out_shape=jax.ShapeDtypeStruct((depth, cond.shape[0], n), F32),
        compiler_params=_params(2),
        name="ada_mod",
    )(cond, w_ada, b_ada.reshape(depth, 1, n))


def _residual_ffn(x, y, mod_ref, gpost_ref, gfpre_ref, gfpost_ref, win_ref, wout_ref):
    x = x + mod_ref[2:3, :] * _rms(y, gpost_ref[...])
    h = _modulated(x, gfpre_ref[...], mod_ref[3:4, :], mod_ref[4:5, :]).astype(BF16)
    gu = jnp.dot(h, win_ref[...], preferred_element_type=F32)
    f = wout_ref.shape[0]
    a = (jax.nn.silu(gu[:, :f]) * gu[:, f:]).astype(BF16)
    y2 = jnp.dot(a, wout_ref[...], preferred_element_type=F32)
    return x + mod_ref[5:6, :] * _rms(y2, gfpost_ref[...])


def _mod_spec(d, n_lat_blocks):
    return pl.BlockSpec((None, None, N_MOD, d),
                        lambda b, r: (b, jnp.minimum(r // n_lat_blocks, 1), 0, 0))


def _qkv_kernel(x_ref, mod_ref, g_ref, wqk_ref, wvt_ref, cos_ref, sa_ref, sb_ref,
                qk_ref, vt_ref, *, rot, q_scale):
    h = _modulated(x_ref[...], g_ref[...], mod_ref[0:1, :], mod_ref[1:2, :]).astype(BF16)
    qk = jnp.dot(h, wqk_ref[...], preferred_element_type=F32)
    cos, sa, sb = cos_ref[...], sa_ref[...], sb_ref[...]
    n_blk = qk.shape[1] // LANES
    for c in range(n_blk):
        blk = qk[:, c * LANES:(c + 1) * LANES]
        y = blk * cos + pltpu.roll(blk, LANES - rot, 1) * sa + pltpu.roll(blk, rot, 1) * sb
        if c < n_blk // 2:
            y = y * q_scale
        qk_ref[c] = y.astype(BF16)
    vt = lax.dot_general(wvt_ref[...], h, (((1,), (1,)), ((), ())), preferred_element_type=F32)
    n_heads, dv_ext, tm = vt_ref.shape
    dv = vt.shape[0] // n_heads
    pad_row = lax.broadcasted_iota(jnp.int32, (dv_ext - dv, tm), 0)
    pad = jnp.where(pad_row == 0, 1.0, 0.0).astype(BF16)
    for hd in range(n_heads):
        vt_ref[hd, :dv, :] = vt[hd * dv:(hd + 1) * dv, :].astype(BF16)
        vt_ref[hd, dv:, :] = pad


def _qkv_call(xt, mods, g_pre, wqk, wvt, cos, sa, sb, *, n_lat_blocks, rot, dk, n_heads):
    b, t, d = xt.shape
    tm = ROW_BLOCK
    n_qk = wqk.shape[1] // LANES
    dv_ext = wvt.shape[0] // n_heads + V_PAD_ROWS
    return pl.pallas_call(
        functools.partial(_qkv_kernel, rot=rot, q_scale=dk ** -0.5 * math.log2(math.e)),
        grid=(b, t // tm),
        in_specs=[
            pl.BlockSpec((None, tm, d), lambda bi, r: (bi, r, 0)),
            _mod_spec(d, n_lat_blocks),
            _const_spec((1, d)),
            _const_spec(wqk.shape),
            _const_spec(wvt.shape),
            pl.BlockSpec((tm, LANES), lambda bi, r: (r, 0)),
            pl.BlockSpec((tm, LANES), lambda bi, r: (r, 0)),
            pl.BlockSpec((tm, LANES), lambda bi, r: (r, 0)),
        ],
        out_specs=[
            pl.BlockSpec((None, n_qk, tm, LANES), lambda bi, r: (bi, 0, r, 0)),
            pl.BlockSpec((None, n_heads, dv_ext, tm), lambda bi, r: (bi, 0, 0, r)),
        ],
        out_shape=[
            jax.ShapeDtypeStruct((b, n_qk, t, LANES), BF16),
            jax.ShapeDtypeStruct((b, n_heads, dv_ext, t), BF16),
        ],
        compiler_params=_params(2),
        name="qkv_rope",
    )(xt, mods, g_pre, wqk, wvt, cos, sa, sb)


def _attn_kernel(q_ref, k_ref, vt_ref, lam_ref, gsub_ref, o_ref, *scratch,
                 dk, tq, first_block, n_ring_blocks, lam_init):
    lam = (jnp.exp(jnp.sum(lam_ref[0:1, :] * lam_ref[1:2, :], keepdims=True))
           - jnp.exp(jnp.sum(lam_ref[2:3, :] * lam_ref[3:4, :], keepdims=True)) + lam_init)
    dv = o_ref.shape[1]

    def query_block(rows, qst_ref, m_ref, acc_ref, *bufs):
        first_buf = bufs[0:2]
        ring = tuple(bufs[i:i + 2] for i in range(2, len(bufs), 2))

        qt = q_ref[rows, :].astype(F32).T
        row = lax.broadcasted_iota(jnp.int32, qt.shape, 0)
        qst_ref[:, :tq] = jnp.where(row < dk, qt, 0.0).astype(BF16)
        qst_ref[:, tq:] = jnp.where(row >= dk, qt, 0.0).astype(BF16)
        m_ref[...] = jnp.full(m_ref.shape, -jnp.inf, F32)
        acc_ref[...] = jnp.zeros(acc_ref.shape, F32)

        def scores(start, size, buf):
            s_ref, mx_ref = buf
            s = jnp.dot(k_ref[pl.ds(start, size), :], qst_ref[...],
                        preferred_element_type=F32)
            s_ref[...] = s
            mx_ref[...] = jnp.max(s, axis=0, keepdims=True)

        def absorb(start, size, buf):
            s_ref, mx_ref = buf
            m_old = m_ref[...]
            m_new = jnp.maximum(m_old, mx_ref[...])
            alpha = jnp.exp2(m_old - m_new)
            p = jnp.exp2(s_ref[...] - m_new)
            pv = jnp.dot(vt_ref[:, pl.ds(start, size)], p.astype(BF16),
                         preferred_element_type=F32)
            acc_ref[...] = alpha * acc_ref[...] + pv
            m_ref[...] = m_new

        scores(*first_block, first_buf)
        ahead = min(len(ring) - 1, n_ring_blocks)
        for blk in range(ahead):
            scores(blk * K_BLOCK, K_BLOCK, ring[blk])
        absorb(*first_block, first_buf)

        if n_ring_blocks:
            def body(j, carry):
                base = pl.multiple_of(j * (K_UNROLL * K_BLOCK), K_UNROLL * K_BLOCK)
                for u in range(K_UNROLL):
                    scores(base + (u + ahead) * K_BLOCK, K_BLOCK, ring[(u + ahead) % len(ring)])
                    absorb(base + u * K_BLOCK, K_BLOCK, ring[u % len(ring)])
                return carry
            n_groups = (n_ring_blocks - ahead) // K_UNROLL
            lax.fori_loop(0, n_groups, body, 0)

            for blk in range(n_groups * K_UNROLL, n_ring_blocks):
                if blk + ahead < n_ring_blocks:
                    scores((blk + ahead) * K_BLOCK, K_BLOCK, ring[(blk + ahead) % len(ring)])
                absorb(blk * K_BLOCK, K_BLOCK, ring[blk % len(ring)])

        acc = acc_ref[:dv, :]
        l = acc_ref[dv:dv + 1, :]
        o = acc[:, :tq] / l[:, :tq] - lam * (acc[:, tq:] / l[:, tq:])
        o_ref[rows, :] = (_rms(o.T, gsub_ref[...]) * (1.0 - lam_init)).astype(BF16)

    n_sub = q_ref.shape[0] // tq
    per_sub = len(scratch) // n_sub
    for sub in range(n_sub):
        query_block(pl.ds(sub * tq, tq), *scratch[sub * per_sub:(sub + 1) * per_sub])


def _attn_call(qk, vt, lam_vecs, g_subln, *, n_heads, dk, n_lat, n_ctx, ctx_queries, lam_init):
    b, _, t, _ = qk.shape
    dv_ext = vt.shape[2]
    dv = dv_ext - V_PAD_ROWS
    if ctx_queries:
        tq, n_sub, n_steps, q_off, n_keys, k_off = n_ctx, 1, 1, n_lat // n_ctx, n_ctx, n_lat // n_ctx
        first_block, n_ring_blocks = (0, n_ctx), 0
    else:
        tq, n_sub, n_steps, q_off, n_keys, k_off = (
            Q_BLOCK, Q_PER_STEP, n_lat // (Q_BLOCK * Q_PER_STEP), 0, t, 0)
        first_block, n_ring_blocks = (n_lat, n_ctx), n_lat // K_BLOCK
    rows = tq * n_sub
    kern = functools.partial(_attn_kernel, dk=dk, tq=tq, first_block=first_block,
                             n_ring_blocks=n_ring_blocks, lam_init=lam_init)

    def score_bufs(n_keys):
        return [pltpu.VMEM((n_keys, 2 * tq), F32), pltpu.VMEM((1, 2 * tq), F32)]

    scratch_set = [
        pltpu.VMEM((2 * dk, 2 * tq), BF16),
        pltpu.VMEM((1, 2 * tq), F32),
        pltpu.VMEM((dv_ext, 2 * tq), F32),
        *score_bufs(first_block[1]),
        *((K_RING * score_bufs(K_BLOCK)) if n_ring_blocks else []),
    ]
    return pl.pallas_call(
        kern,
        grid=(b, n_heads, n_steps),
        in_specs=[
            pl.BlockSpec((None, None, rows, LANES), lambda bi, h, i: (bi, h, q_off + i, 0)),
            pl.BlockSpec((None, None, n_keys, LANES), lambda bi, h, i: (bi, n_heads + h, k_off, 0)),
            pl.BlockSpec((None, None, dv_ext, n_keys), lambda bi, h, i: (bi, h, 0, k_off)),
            pl.BlockSpec(lam_vecs.shape, lambda bi, h, i: (0, 0)),
            pl.BlockSpec((1, dv), lambda bi, h, i: (0, 0)),
        ],
        out_specs=pl.BlockSpec((None, rows, dv), lambda bi, h, i: (bi, i, h)),
        out_shape=jax.ShapeDtypeStruct((b, n_steps * rows, n_heads * dv), BF16),
        scratch_shapes=n_sub * scratch_set,
        compiler_params=_params(3),
        name="diff_attn_ctx" if ctx_queries else "diff_attn",
    )(qk, qk, vt, lam_vecs, g_subln)


def _attn_out_ffn_kernel(o_ref, x_ref, mod_ref, gpost_ref, gfpre_ref, gfpost_ref,
                         wo_ref, win_ref, wout_ref, out_ref):
    y = jnp.dot(o_ref[...], wo_ref[...], preferred_element_type=F32)
    out_ref[...] = _residual_ffn(x_ref[...], y, mod_ref, gpost_ref, gfpre_ref, gfpost_ref,
                                 win_ref, wout_ref)


def _attn_out_ffn_call(o, xt, mods, g_post, g_fpre, g_fpost, wo, win, wout, *, n_lat_blocks):
    b, rows, d = o.shape[0], o.shape[1], xt.shape[2]
    tm = ROW_BLOCK
    return pl.pallas_call(
        _attn_out_ffn_kernel,
        grid=(b, rows // tm),
        in_specs=[
            pl.BlockSpec((None, tm, o.shape[2]), lambda bi, r: (bi, r, 0)),
            pl.BlockSpec((None, tm, d), lambda bi, r: (bi, r, 0)),
            _mod_spec(d, n_lat_blocks),
            _const_spec((1, d)), _const_spec((1, d)), _const_spec((1, d)),
            _const_spec(wo.shape), _const_spec(win.shape), _const_spec(wout.shape),
        ],
        out_specs=pl.BlockSpec((None, tm, d), lambda bi, r: (bi, r, 0)),
        out_shape=jax.ShapeDtypeStruct((b, rows, d), F32),
        compiler_params=_params(2),
        name="attn_out_ffn",
    )(o, xt, mods, g_post, g_fpre, g_fpost, wo, win, wout)


def _cmlp_ffn_kernel(x_ref, mod_ref, gpre_ref, wuv_ref, buv_ref, gsgu_ref, bsgu_ref,
                     wsp_ref, bsp_ref, wco_ref, gpost_ref, gfpre_ref, gfpost_ref,
                     win_ref, wout_ref, out_ref):
    x = x_ref[...]
    h = _modulated(x, gpre_ref[...], mod_ref[0:1, :], mod_ref[1:2, :]).astype(BF16)
    z = jnp.dot(h, wuv_ref[...], preferred_element_type=F32) + buv_ref[...]
    z = 0.5 * z * (1.0 + lax.erf(z * math.sqrt(0.5)))
    w = z.shape[1] // 2
    u, v = z[:, :w], z[:, w:]
    mu = jnp.mean(v, axis=-1, keepdims=True)
    vc = v - mu
    var = jnp.mean(vc * vc, axis=-1, keepdims=True)
    vb = (vc * lax.rsqrt(var + EPS) * gsgu_ref[...] + bsgu_ref[...]).astype(BF16)

    groups, chunk, _ = wsp_ref.shape
    gd = w // groups
    n_chunks = x.shape[0] // chunk
    mixed = []
    for g in range(groups):
        rhs = jnp.concatenate(
            [vb[n * chunk:(n + 1) * chunk, g * gd:(g + 1) * gd] for n in range(n_chunks)], axis=1)
        mixed.append(jnp.dot(wsp_ref[g], rhs, preferred_element_type=F32) + bsp_ref[g])
    sv = jnp.concatenate(
        [jnp.concatenate([mixed[g][:, n * gd:(n + 1) * gd] for g in range(groups)], axis=1)
         for n in range(n_chunks)], axis=0)
    y = jnp.dot((u * sv).astype(BF16), wco_ref[...], preferred_element_type=F32)
    out_ref[...] = _residual_ffn(x, y, mod_ref, gpost_ref, gfpre_ref, gfpost_ref,
                                 win_ref, wout_ref)


def _cmlp_ffn_call(xt, mods, g_pre, wuv, buv, g_sgu, b_sgu, wsp, bsp, wco,
                   g_post, g_fpre, g_fpost, win, wout, *, n_lat_blocks):
    b, rows, d = xt.shape
    tm = ROW_BLOCK
    vec = lambda a: _const_spec((1, a.shape[-1]))
    return pl.pallas_call(
        _cmlp_ffn_kernel,
        grid=(b, rows // tm),
        in_specs=[
            pl.BlockSpec((None, tm, d), lambda bi, r: (bi, r, 0)),
            _mod_spec(d, n_lat_blocks),
            vec(g_pre), _const_spec(wuv.shape), vec(buv), vec(g_sgu), vec(b_sgu),
            _const_spec(wsp.shape), _const_spec(bsp.shape), _const_spec(wco.shape),
            vec(g_post), vec(g_fpre), vec(g_fpost),
            _const_spec(win.shape), _const_spec(wout.shape),
        ],
        out_specs=pl.BlockSpec((None, tm, d), lambda bi, r: (bi, r, 0)),
        out_shape=jax.ShapeDtypeStruct((b, rows, d), F32),
        compiler_params=_params(2),
        name="cmlp_ffn",
    )(xt, mods, g_pre, wuv, buv, g_sgu, b_sgu, wsp, bsp, wco, g_post, g_fpre, g_fpost, win, wout)


def _rope_tables(n_lat, n_ctx, dk):
    axis_dim = dk // 2
    rot = axis_dim // 2
    tok = jnp.arange(n_lat, dtype=jnp.int32)
    row = (tok // ROPE_GRID_W).astype(F32)
    col = (tok % ROPE_GRID_W).astype(F32)
    inv_freq = ROPE_BASE ** (-jnp.arange(0, axis_dim, 2, dtype=F32) / axis_dim)
    lane = jnp.arange(LANES, dtype=jnp.int32)
    in_map = lane % dk
    freq = inv_freq[in_map % rot]
    pos = jnp.where((in_map < axis_dim)[None, :], row[:, None], col[:, None])
    ang = pos * freq[None, :]
    first = ((in_map % axis_dim) < rot)[None, :]
    cos, sin = jnp.cos(ang), jnp.sin(ang)
    sa = jnp.where(first, -sin, 0.0)
    sb = jnp.where(first, 0.0, sin)
    pad = lambda a, v: jnp.concatenate([a, jnp.full((n_ctx, LANES), v, F32)], axis=0)
    return pad(cos, 1.0), pad(sa, 0.0), pad(sb, 0.0), rot


def kernel(x, c, ctx, c_ctx, w_ada, b_ada, g_mix_pre, g_mix_post, g_ffn_pre, g_ffn_post, w_qkv, w_attn_out, lam_q1, lam_k1, lam_q2, lam_k2, g_subln, w_uv, b_uv, g_sgu, b_sgu, w_spatial, b_spatial, w_cmlp_out, w_ffn_in, w_ffn_out):
    b, s, d = x.shape
    n_ctx = ctx.shape[1]
    depth = w_ada.shape[0]
    dk = lam_q1.shape[-1]
    dv = g_subln.shape[-1]
    n_heads = w_attn_out.shape[1] // dv
    qk_w = 2 * n_heads * dk * 2
    assert 2 * dk == LANES and dv == LANES
    assert K_UNROLL % K_RING == 0 and s % K_BLOCK == 0 and s % (Q_BLOCK * Q_PER_STEP) == 0 and s % n_ctx == 0
    assert s % ROW_BLOCK == 0 and n_ctx % ROW_BLOCK == 0 and b + 1 <= SUBLANES

    cond = jnp.zeros((SUBLANES, d), F32).at[:b].set(c).at[b].set(c_ctx)
    mod_all = _ada_call(cond, w_ada, b_ada)
    mod_lat = mod_all[:, :b].reshape(depth, b, 1, N_MOD, d)
    mod_ctx = jnp.broadcast_to(mod_all[:, b].reshape(depth, 1, 1, N_MOD, d), mod_lat.shape)
    mods = jnp.concatenate([mod_lat, mod_ctx], axis=2)

    cos, sa, sb, rot = _rope_tables(s, n_ctx, dk)
    n_lat_blocks = s // ROW_BLOCK
    last_attn = max(i for i in range(depth) if i % N_MIXERS == 0)
    row_vec = lambda a: a.reshape(1, -1)

    xt = jnp.concatenate([x, ctx], axis=1)
    for i in range(depth):
        j = i // N_MIXERS
        ctx_live = i < last_attn
        win = w_ffn_in[i].astype(BF16)
        wout = w_ffn_out[i].astype(BF16)
        tail = (row_vec(g_mix_post[i]), row_vec(g_ffn_pre[i]), row_vec(g_ffn_post[i]))
        if i % N_MIXERS == 0:
            lam_init = 0.8 - 0.6 * math.exp(-0.3 * i)
            wqk = w_qkv[j][:, :qk_w].astype(BF16)
            wvt = w_qkv[j][:, qk_w:].T.astype(BF16)
            qk, vt = _qkv_call(xt, mods[i], row_vec(g_mix_pre[i]), wqk, wvt, cos, sa, sb,
                               n_lat_blocks=n_lat_blocks, rot=rot, dk=dk, n_heads=n_heads)
            lam_vecs = jnp.stack([lam_q1[j], lam_k1[j], lam_q2[j], lam_k2[j]]).astype(F32)
            attend = functools.partial(_attn_call, qk, vt, lam_vecs, row_vec(g_subln[j]),
                                       n_heads=n_heads, dk=dk, n_lat=s, n_ctx=n_ctx,
                                       lam_init=lam_init)
            o = attend(ctx_queries=False)
            if ctx_live:
                o = jnp.concatenate([o, attend(ctx_queries=True)], axis=1)
            xt = _attn_out_ffn_call(o, xt, mods[i], *tail, w_attn_out[j].astype(BF16), win, wout,
                                    n_lat_blocks=n_lat_blocks)
        else:
            if not ctx_live and xt.shape[1] != s:
                xt = xt[:, :s]
            xt = _cmlp_ffn_call(
                xt, mods[i], row_vec(g_mix_pre[i]), w_uv[j].astype(BF16), row_vec(b_uv[j]),
                row_vec(g_sgu[j]), row_vec(b_sgu[j]), w_spatial[j].astype(BF16),
                b_spatial[j][:, :, None], w_cmlp_out[j].astype(BF16), *tail, win, wout,
                n_lat_blocks=n_lat_blocks)
    return xt[:, :s] if xt.shape[1] != s else xt
```

```python
import functools
import math

import jax
import jax.numpy as jnp
from jax import lax
from jax.experimental import pallas as pl
from jax.experimental.pallas import tpu as pltpu

F32 = jnp.float32
BF16 = jnp.bfloat16

EPS = 1e-6
ROPE_GRID_W = 64
ROPE_BASE = 10000.0
N_MIXERS = 2
N_MOD = 6

LANES = 128
SUBLANES = 8
VMEM_LIMIT_BYTES = 56 << 20

ROW_BLOCK = 256
ROW_GROUPS = 2
Q_BLOCK = 256
Q_PER_STEP = 2
K_BLOCK = 1024
V_PAD_ROWS = 16
K_RING = 3
K_UNROLL = 6


def _const_spec(shape):
    return pl.BlockSpec(shape, lambda *_: (0,) * len(shape), pipeline_mode=pl.Buffered(1))


def _params(n_grid, flags=None):
    return pltpu.CompilerParams(
        dimension_semantics=("arbitrary",) * n_grid, vmem_limit_bytes=VMEM_LIMIT_BYTES,
        flags=flags)


def _rms(x, g):
    return x * lax.rsqrt(jnp.mean(x * x, axis=-1, keepdims=True) + EPS) * g


def _modulated(x, g, shift, scale):
    return _rms(x, g) * (1.0 + scale) + shift


def _ada_kernel(c_ref, w_ref, b_ref, o_ref):
    s = jax.nn.silu(c_ref[...])
    o_ref[...] = jnp.dot(s, w_ref[...], preferred_element_type=F32,
                         precision=lax.Precision.HIGHEST) + b_ref[...]


def _ada_call(cond, w_ada, b_ada):
    depth, d, n = w_ada.shape
    tn = 2048
    return pl.pallas_call(
        _ada_kernel,
        grid=(depth, n // tn),
        in_specs=[
            pl.BlockSpec(cond.shape, lambda l, k: (0, 0)),
            pl.BlockSpec((None, d, tn), lambda l, k: (l, 0, k)),
            pl.BlockSpec((None, 1, tn), lambda l, k: (l, 0, k)),
        ],
        out_specs=pl.BlockSpec((None, cond.shape[0], tn), lambda l, k: (l, 0, k)),
        out_shape=jax.ShapeDtypeStruct((depth, cond.shape[0], n), F32),
        compiler_params=_params(2),
        name="ada_mod",
    )(cond, w_ada, b_ada.reshape(depth, 1, n))


def _residual_ffn(x, y, mod_ref, gpost_ref, gfpre_ref, gfpost_ref, win_ref, wout_ref):
    x = x + mod_ref[2:3, :] * _rms(y, gpost_ref[...])
    h = _modulated(x, gfpre_ref[...], mod_ref[3:4, :], mod_ref[4:5, :]).astype(BF16)
    gu = jnp.dot(h, win_ref[...], preferred_element_type=F32)
    f = wout_ref.shape[0]
    a = (jax.nn.silu(gu[:, :f]) * gu[:, f:]).astype(BF16)
    y2 = jnp.dot(a, wout_ref[...], preferred_element_type=F32)
    return x + mod_ref[5:6, :] * _rms(y2, gfpost_ref[...])


def _row_block(rows, n_lat):
    wide = ROW_BLOCK * ROW_GROUPS
    return wide if rows == n_lat and rows % wide == 0 else ROW_BLOCK


def _mod_spec(d, n_lat_blocks):
    return pl.BlockSpec((None, None, N_MOD, d),
                        lambda b, r: (b, jnp.minimum(r // n_lat_blocks, 1), 0, 0))


def _qkv_kernel(x_ref, mod_ref, g_ref, wqk_ref, wvt_ref, cos_ref, sa_ref, sb_ref,
                qk_ref, vt_ref, *, rot, q_scale):
    h = _modulated(x_ref[...], g_ref[...], mod_ref[0:1, :], mod_ref[1:2, :]).astype(BF16)
    qk = jnp.dot(h, wqk_ref[...], preferred_element_type=F32)
    cos, sa, sb = cos_ref[...], sa_ref[...], sb_ref[...]
    n_blk = qk.shape[1] // LANES
    for c in range(n_blk):
        blk = qk[:, c * LANES:(c + 1) * LANES]
        y = blk * cos + pltpu.roll(blk, LANES - rot, 1) * sa + pltpu.roll(blk, rot, 1) * sb
        if c < n_blk // 2:
            y = y * q_scale
        qk_ref[c] = y.astype(BF16)
    vt = lax.dot_general(wvt_ref[...], h, (((1,), (1,)), ((), ())), preferred_element_type=F32)
    n_heads, dv_ext, tm = vt_ref.shape
    dv = vt.shape[0] // n_heads
    pad_row = lax.broadcasted_iota(jnp.int32, (dv_ext - dv, tm), 0)
    pad = jnp.where(pad_row == 0, 1.0, 0.0).astype(BF16)
    for hd in range(n_heads):
        vt_ref[hd, :dv, :] = vt[hd * dv:(hd + 1) * dv, :].astype(BF16)
        vt_ref[hd, dv:, :] = pad


def _qkv_call(xt, mods, g_pre, wqk, wvt, cos, sa, sb, *, n_lat, rot, dk, n_heads):
    b, t, d = xt.shape
    tm = ROW_BLOCK
    n_lat_blocks = n_lat // tm
    n_qk = wqk.shape[1] // LANES
    dv_ext = wvt.shape[0] // n_heads + V_PAD_ROWS
    return pl.pallas_call(
        functools.partial(_qkv_kernel, rot=rot, q_scale=dk ** -0.5 * math.log2(math.e)),
        grid=(b, t // tm),
        in_specs=[
            pl.BlockSpec((None, tm, d), lambda bi, r: (bi, r, 0)),
            _mod_spec(d, n_lat_blocks),
            _const_spec((1, d)),
            _const_spec(wqk.shape),
            _const_spec(wvt.shape),
            pl.BlockSpec((tm, LANES), lambda bi, r: (r, 0)),
            pl.BlockSpec((tm, LANES), lambda bi, r: (r, 0)),
            pl.BlockSpec((tm, LANES), lambda bi, r: (r, 0)),
        ],
        out_specs=[
            pl.BlockSpec((None, n_qk, tm, LANES), lambda bi, r: (bi, 0, r, 0)),
            pl.BlockSpec((None, n_heads, dv_ext, tm), lambda bi, r: (bi, 0, 0, r)),
        ],
        out_shape=[
            jax.ShapeDtypeStruct((b, n_qk, t, LANES), BF16),
            jax.ShapeDtypeStruct((b, n_heads, dv_ext, t), BF16),
        ],
        compiler_params=_params(2),
        name="qkv_rope",
    )(xt, mods, g_pre, wqk, wvt, cos, sa, sb)


def _attn_kernel(q_ref, k_ref, vt_ref, lam_ref, gsub_ref, o_ref, *scratch,
                 dk, tq, first_block, n_ring_blocks, lam_init):
    lam = (jnp.exp(jnp.sum(lam_ref[0:1, :] * lam_ref[1:2, :], keepdims=True))
           - jnp.exp(jnp.sum(lam_ref[2:3, :] * lam_ref[3:4, :], keepdims=True)) + lam_init)
    dv = o_ref.shape[1]

    class QueryBlock:
        def __init__(self, rows, qst_ref, m_ref, acc_ref, *bufs):
            self.rows, self.qst_ref, self.m_ref, self.acc_ref = rows, qst_ref, m_ref, acc_ref
            self.first_buf = bufs[0:2]
            self.ring = tuple(bufs[i:i + 2] for i in range(2, len(bufs), 2))
            self.ahead = min(len(self.ring) - 1, n_ring_blocks)

        def start(self):
            qt = q_ref[self.rows, :].astype(F32).T
            row = lax.broadcasted_iota(jnp.int32, qt.shape, 0)
            self.qst_ref[:, :tq] = jnp.where(row < dk, qt, 0.0).astype(BF16)
            self.qst_ref[:, tq:] = jnp.where(row >= dk, qt, 0.0).astype(BF16)
            self.m_ref[...] = jnp.full(self.m_ref.shape, -jnp.inf, F32)
            self.acc_ref[...] = jnp.zeros(self.acc_ref.shape, F32)
            self.scores(*first_block, self.first_buf)
            for blk in range(self.ahead):
                self.scores(blk * K_BLOCK, K_BLOCK, self.ring[blk])
            self.absorb(*first_block, self.first_buf)

        def scores(self, start, size, buf):
            s_ref, mx_ref = buf
            s = jnp.dot(k_ref[pl.ds(start, size), :], self.qst_ref[...],
                        preferred_element_type=F32)
            s_ref[...] = s
            mx_ref[...] = jnp.max(s, axis=0, keepdims=True)

        def absorb(self, start, size, buf):
            s_ref, mx_ref = buf
            m_old = self.m_ref[...]
            m_new = jnp.maximum(m_old, mx_ref[...])
            alpha = jnp.exp2(m_old - m_new)
            p = jnp.exp2(s_ref[...] - m_new)
            pv = jnp.dot(vt_ref[:, pl.ds(start, size)], p.astype(BF16),
                         preferred_element_type=F32)
            self.acc_ref[...] = alpha * self.acc_ref[...] + pv
            self.m_ref[...] = m_new

        def ring_step(self, base, blk, last_scored):
            n = len(self.ring)
            if blk + self.ahead <= last_scored:
                self.scores(base + (blk + self.ahead) * K_BLOCK, K_BLOCK,
                            self.ring[(blk + self.ahead) % n])
            self.absorb(base + blk * K_BLOCK, K_BLOCK, self.ring[blk % n])

        def finish(self):
            acc = self.acc_ref[:dv, :]
            l = self.acc_ref[dv:dv + 1, :]
            o = acc[:, :tq] / l[:, :tq] - lam * (acc[:, tq:] / l[:, tq:])
            o_ref[self.rows, :] = (_rms(o.T, gsub_ref[...]) * (1.0 - lam_init)).astype(BF16)

    n_sub = q_ref.shape[0] // tq
    per_sub = len(scratch) // n_sub
    blocks = [QueryBlock(pl.ds(sub * tq, tq), *scratch[sub * per_sub:(sub + 1) * per_sub])
              for sub in range(n_sub)]
    for qb in blocks:
        qb.start()
    if n_ring_blocks:
        ahead = blocks[0].ahead

        def body(j, carry):
            base = pl.multiple_of(j * (K_UNROLL * K_BLOCK), K_UNROLL * K_BLOCK)
            for u in range(K_UNROLL):
                for qb in blocks:
                    qb.ring_step(base, u, K_UNROLL + ahead)
            return carry
        n_groups = (n_ring_blocks - ahead) // K_UNROLL
        lax.fori_loop(0, n_groups, body, 0)
        for blk in range(n_groups * K_UNROLL, n_ring_blocks):
            for qb in blocks:
                qb.ring_step(0, blk, n_ring_blocks - 1)
    for qb in blocks:
        qb.finish()


def _attn_call(qk, vt, lam_vecs, g_subln, *, n_heads, dk, n_lat, n_ctx, ctx_queries, lam_init):
    b, _, t, _ = qk.shape
    dv_ext = vt.shape[2]
    dv = dv_ext - V_PAD_ROWS
    if ctx_queries:
        tq, n_sub, n_steps, q_off, n_keys, k_off = n_ctx, 1, 1, n_lat // n_ctx, n_ctx, n_lat // n_ctx
        first_block, n_ring_blocks = (0, n_ctx), 0
    else:
        tq, n_sub, n_steps, q_off, n_keys, k_off = (
            Q_BLOCK, Q_PER_STEP, n_lat // (Q_BLOCK * Q_PER_STEP), 0, t, 0)
        first_block, n_ring_blocks = (n_lat, n_ctx), n_lat // K_BLOCK
    rows = tq * n_sub
    kern = functools.partial(_attn_kernel, dk=dk, tq=tq, first_block=first_block,
                             n_ring_blocks=n_ring_blocks, lam_init=lam_init)

    def score_bufs(n_keys):
        return [pltpu.VMEM((n_keys, 2 * tq), F32), pltpu.VMEM((1, 2 * tq), F32)]

    scratch_set = [
        pltpu.VMEM((2 * dk, 2 * tq), BF16),
        pltpu.VMEM((1, 2 * tq), F32),
        pltpu.VMEM((dv_ext, 2 * tq), F32),
        *score_bufs(first_block[1]),
        *((K_RING * score_bufs(K_BLOCK)) if n_ring_blocks else []),
    ]
    return pl.pallas_call(
        kern,
        grid=(b, n_heads, n_steps),
        in_specs=[
            pl.BlockSpec((None, None, rows, LANES), lambda bi, h, i: (bi, h, q_off + i, 0)),
            pl.BlockSpec((None, None, n_keys, LANES), lambda bi, h, i: (bi, n_heads + h, k_off, 0)),
            pl.BlockSpec((None, None, dv_ext, n_keys), lambda bi, h, i: (bi, h, 0, k_off)),
            pl.BlockSpec(lam_vecs.shape, lambda bi, h, i: (0, 0)),
            pl.BlockSpec((1, dv), lambda bi, h, i: (0, 0)),
        ],
        out_specs=pl.BlockSpec((None, rows, dv), lambda bi, h, i: (bi, i, h)),
        out_shape=jax.ShapeDtypeStruct((b, n_steps * rows, n_heads * dv), BF16),
        scratch_shapes=n_sub * scratch_set,
        compiler_params=_params(3),
        name="diff_attn_ctx" if ctx_queries else "diff_attn",
    )(qk, qk, vt, lam_vecs, g_subln)


def _attn_out_ffn_kernel(o_ref, x_ref, mod_ref, gpost_ref, gfpre_ref, gfpost_ref,
                         wo_ref, win_ref, wout_ref, out_ref):
    for g in range(x_ref.shape[0] // ROW_BLOCK):
        rows = pl.ds(g * ROW_BLOCK, ROW_BLOCK)
        y = jnp.dot(o_ref[rows, :], wo_ref[...], preferred_element_type=F32)
        out_ref[rows, :] = _residual_ffn(x_ref[rows, :], y, mod_ref, gpost_ref, gfpre_ref,
                                         gfpost_ref, win_ref, wout_ref)


def _attn_out_ffn_call(o, xt, mods, g_post, g_fpre, g_fpost, wo, win, wout, *, n_lat):
    b, rows, d = o.shape[0], o.shape[1], xt.shape[2]
    tm = _row_block(rows, n_lat)
    n_lat_blocks = n_lat // tm
    return pl.pallas_call(
        _attn_out_ffn_kernel,
        grid=(b, rows // tm),
        in_specs=[
            pl.BlockSpec((None, tm, o.shape[2]), lambda bi, r: (bi, r, 0)),
            pl.BlockSpec((None, tm, d), lambda bi, r: (bi, r, 0)),
            _mod_spec(d, n_lat_blocks),
            _const_spec((1, d)), _const_spec((1, d)), _const_spec((1, d)),
            _const_spec(wo.shape), _const_spec(win.shape), _const_spec(wout.shape),
        ],
        out_specs=pl.BlockSpec((None, tm, d), lambda bi, r: (bi, r, 0)),
        out_shape=jax.ShapeDtypeStruct((b, rows, d), F32),
        compiler_params=_params(2),
        name="attn_out_ffn",
    )(o, xt, mods, g_post, g_fpre, g_fpost, wo, win, wout)


def _cmlp_ffn_kernel(x_ref, mod_ref, gpre_ref, wuv_ref, buv_ref, gsgu_ref, bsgu_ref,
                     wsp_ref, bsp_ref, wco_ref, gpost_ref, gfpre_ref, gfpost_ref,
                     win_ref, wout_ref, out_ref):
    groups, chunk, _ = wsp_ref.shape
    for rg in range(x_ref.shape[0] // ROW_BLOCK):
        rows = pl.ds(rg * ROW_BLOCK, ROW_BLOCK)
        x = x_ref[rows, :]
        h = _modulated(x, gpre_ref[...], mod_ref[0:1, :], mod_ref[1:2, :]).astype(BF16)
        z = jnp.dot(h, wuv_ref[...], preferred_element_type=F32) + buv_ref[...]
        z = 0.5 * z * (1.0 + lax.erf(z * math.sqrt(0.5)))
        w = z.shape[1] // 2
        u, v = z[:, :w], z[:, w:]
        mu = jnp.mean(v, axis=-1, keepdims=True)
        vc = v - mu
        var = jnp.mean(vc * vc, axis=-1, keepdims=True)
        vb = (vc * lax.rsqrt(var + EPS) * gsgu_ref[...] + bsgu_ref[...]).astype(BF16)

        gd = w // groups
        n_chunks = ROW_BLOCK // chunk
        mixed = []
        for g in range(groups):
            rhs = jnp.concatenate(
                [vb[n * chunk:(n + 1) * chunk, g * gd:(g + 1) * gd] for n in range(n_chunks)],
                axis=1)
            mixed.append(jnp.dot(wsp_ref[g], rhs, preferred_element_type=F32) + bsp_ref[g])
        sv = jnp.concatenate(
            [jnp.concatenate([mixed[g][:, n * gd:(n + 1) * gd] for g in range(groups)], axis=1)
             for n in range(n_chunks)], axis=0)
        y = jnp.dot((u * sv).astype(BF16), wco_ref[...], preferred_element_type=F32)
        out_ref[rows, :] = _residual_ffn(x, y, mod_ref, gpost_ref, gfpre_ref, gfpost_ref,
                                         win_ref, wout_ref)


def _cmlp_ffn_call(xt, mods, g_pre, wuv, buv, g_sgu, b_sgu, wsp, bsp, wco,
                   g_post, g_fpre, g_fpost, win, wout, *, n_lat):
    b, rows, d = xt.shape
    tm = _row_block(rows, n_lat)
    n_lat_blocks = n_lat // tm
    vec = lambda a: _const_spec((1, a.shape[-1]))
    return pl.pallas_call(
        _cmlp_ffn_kernel,
        grid=(b, rows // tm),
        in_specs=[
            pl.BlockSpec((None, tm, d), lambda bi, r: (bi, r, 0)),
            _mod_spec(d, n_lat_blocks),
            vec(g_pre), _const_spec(wuv.shape), vec(buv), vec(g_sgu), vec(b_sgu),
            _const_spec(wsp.shape), _const_spec(bsp.shape), _const_spec(wco.shape),
            vec(g_post), vec(g_fpre), vec(g_fpost),
            _const_spec(win.shape), _const_spec(wout.shape),
        ],
        out_specs=pl.BlockSpec((None, tm, d), lambda bi, r: (bi, r, 0)),
        out_shape=jax.ShapeDtypeStruct((b, rows, d), F32),
        compiler_params=_params(2),
        name="cmlp_ffn",
    )(xt, mods, g_pre, wuv, buv, g_sgu, b_sgu, wsp, bsp, wco, g_post, g_fpre, g_fpost, win, wout)


def _rope_tables(n_lat, n_ctx, dk):
    axis_dim = dk // 2
    rot = axis_dim // 2
    tok = jnp.arange(n_lat, dtype=jnp.int32)
    row = (tok // ROPE_GRID_W).astype(F32)
    col = (tok % ROPE_GRID_W).astype(F32)
    inv_freq = ROPE_BASE ** (-jnp.arange(0, axis_dim, 2, dtype=F32) / axis_dim)
    lane = jnp.arange(LANES, dtype=jnp.int32)
    in_map = lane % dk
    freq = inv_freq[in_map % rot]
    pos = jnp.where((in_map < axis_dim)[None, :], row[:, None], col[:, None])
    ang = pos * freq[None, :]
    first = ((in_map % axis_dim) < rot)[None, :]
    cos, sin = jnp.cos(ang), jnp.sin(ang)
    sa = jnp.where(first, -sin, 0.0)
    sb = jnp.where(first, 0.0, sin)
    pad = lambda a, v: jnp.concatenate([a, jnp.full((n_ctx, LANES), v, F32)], axis=0)
    return pad(cos, 1.0), pad(sa, 0.0), pad(sb, 0.0), rot


def kernel(x, c, ctx, c_ctx, w_ada, b_ada, g_mix_pre, g_mix_post, g_ffn_pre, g_ffn_post, w_qkv, w_attn_out, lam_q1, lam_k1, lam_q2, lam_k2, g_subln, w_uv, b_uv, g_sgu, b_sgu, w_spatial, b_spatial, w_cmlp_out, w_ffn_in, w_ffn_out):
    b, s, d = x.shape
    n_ctx = ctx.shape[1]
    depth = w_ada.shape[0]
    dk = lam_q1.shape[-1]
    dv = g_subln.shape[-1]
    n_heads = w_attn_out.shape[1] // dv
    qk_w = 2 * n_heads * dk * 2
    assert 2 * dk == LANES and dv == LANES
    assert K_UNROLL % K_RING == 0 and s % K_BLOCK == 0 and s % (Q_BLOCK * Q_PER_STEP) == 0 and s % n_ctx == 0
    assert s % ROW_BLOCK == 0 and n_ctx % ROW_BLOCK == 0 and b + 1 <= SUBLANES

    cond = jnp.zeros((SUBLANES, d), F32).at[:b].set(c).at[b].set(c_ctx)
    mod_all = _ada_call(cond, w_ada, b_ada)
    mod_lat = mod_all[:, :b].reshape(depth, b, 1, N_MOD, d)
    mod_ctx = jnp.broadcast_to(mod_all[:, b].reshape(depth, 1, 1, N_MOD, d), mod_lat.shape)
    mods = jnp.concatenate([mod_lat, mod_ctx], axis=2)

    cos, sa, sb, rot = _rope_tables(s, n_ctx, dk)
    last_attn = max(i for i in range(depth) if i % N_MIXERS == 0)
    row_vec = lambda a: a.reshape(1, -1)

    xt = jnp.concatenate([x, ctx], axis=1)
    for i in range(depth):
        j = i // N_MIXERS
        ctx_live = i < last_attn
        win = w_ffn_in[i].astype(BF16)
        wout = w_ffn_out[i].astype(BF16)
        tail = (row_vec(g_mix_post[i]), row_vec(g_ffn_pre[i]), row_vec(g_ffn_post[i]))
        if i % N_MIXERS == 0:
            lam_init = 0.8 - 0.6 * math.exp(-0.3 * i)
            wqk = w_qkv[j][:, :qk_w].astype(BF16)
            wvt = w_qkv[j][:, qk_w:].T.astype(BF16)
            qk, vt = _qkv_call(xt, mods[i], row_vec(g_mix_pre[i]), wqk, wvt, cos, sa, sb,
                               n_lat=s, rot=rot, dk=dk, n_heads=n_heads)
            lam_vecs = jnp.stack([lam_q1[j], lam_k1[j], lam_q2[j], lam_k2[j]]).astype(F32)
            attend = functools.partial(_attn_call, qk, vt, lam_vecs, row_vec(g_subln[j]),
                                       n_heads=n_heads, dk=dk, n_lat=s, n_ctx=n_ctx,
                                       lam_init=lam_init)
            o = attend(ctx_queries=False)
            if ctx_live:
                o = jnp.concatenate([o, attend(ctx_queries=True)], axis=1)
            xt = _attn_out_ffn_call(o, xt, mods[i], *tail, w_attn_out[j].astype(BF16), win, wout,
                                    n_lat=s)
        else:
            if not ctx_live and xt.shape[1] != s:
                xt = xt[:, :s]
            xt = _cmlp_ffn_call(
                xt, mods[i], row_vec(g_mix_pre[i]), w_uv[j].astype(BF16), row_vec(b_uv[j]),
                row_vec(g_sgu[j]), row_vec(b_sgu[j]), w_spatial[j].astype(BF16),
                b_spatial[j][:, :, None], w_cmlp_out[j].astype(BF16), *tail, win, wout,
                n_lat=s)
    return xt[:, :s] if xt.shape[1] != s else xt
```

```python
import functools
import math

import jax
import jax.numpy as jnp
from jax import lax
from jax.experimental import pallas as pl
from jax.experimental.pallas import tpu as pltpu

F32 = jnp.float32
BF16 = jnp.bfloat16
F8 = jnp.float8_e4m3fn

EPS = 1e-6
MIN_AMAX = 2.0 ** -100
ROPE_GRID_W = 64
ROPE_BASE = 10000.0
N_MIXERS = 2
N_MOD = 6

LANES = 128
SUBLANES = 8
VMEM_LIMIT_BYTES = 56 << 20

ROW_BLOCK = 256
ROW_GROUPS = 2
Q_BLOCK = 256
Q_PER_STEP = 2
K_BLOCK = 1024
V_PAD_ROWS = 16
K_RING = 3
K_UNROLL = 6


def _const_spec(shape):
    return pl.BlockSpec(shape, lambda *_: (0,) * len(shape), pipeline_mode=pl.Buffered(1))


def _params(n_grid, flags=None):
    return pltpu.CompilerParams(
        dimension_semantics=("arbitrary",) * n_grid, vmem_limit_bytes=VMEM_LIMIT_BYTES,
        flags=flags)


def _rms(x, g):
    return x * lax.rsqrt(jnp.mean(x * x, axis=-1, keepdims=True) + EPS) * g


def _modulated(x, g, shift, scale):
    return _rms(x, g) * (1.0 + scale) + shift


def _ada_kernel(c_ref, w_ref, b_ref, o_ref):
    s = jax.nn.silu(c_ref[...])
    o_ref[...] = jnp.dot(s, w_ref[...], preferred_element_type=F32,
                         precision=lax.Precision.HIGHEST) + b_ref[...]


def _ada_call(cond, w_ada, b_ada):
    depth, d, n = w_ada.shape
    tn = 2048
    return pl.pallas_call(
        _ada_kernel,
        grid=(depth, n // tn),
        in_specs=[
            pl.BlockSpec(cond.shape, lambda l, k: (0, 0)),
            pl.BlockSpec((None, d, tn), lambda l, k: (l, 0, k)),
            pl.BlockSpec((None, 1, tn), lambda l, k: (l, 0, k)),
        ],
        out_specs=pl.BlockSpec((None, cond.shape[0], tn), lambda l, k: (l, 0, k)),
        out_shape=jax.ShapeDtypeStruct((depth, cond.shape[0], n), F32),
        compiler_params=_params(2),
        name="ada_mod",
    )(cond, w_ada, b_ada.reshape(depth, 1, n))


def _residual_ffn(x, y, mod_ref, gpost_ref, gfpre_ref, gfpost_ref, win_ref, wout_ref):
    x = x + mod_ref[2:3, :] * _rms(y, gpost_ref[...])
    h = _modulated(x, gfpre_ref[...], mod_ref[3:4, :], mod_ref[4:5, :]).astype(BF16)
    gu = jnp.dot(h, win_ref[...], preferred_element_type=F32)
    f = wout_ref.shape[0]
    a = (jax.nn.silu(gu[:, :f]) * gu[:, f:]).astype(BF16)
    y2 = jnp.dot(a, wout_ref[...], preferred_element_type=F32)
    return x + mod_ref[5:6, :] * _rms(y2, gfpost_ref[...])


def _row_block(rows, n_lat):
    wide = ROW_BLOCK * ROW_GROUPS
    return wide if rows == n_lat and rows % wide == 0 else ROW_BLOCK


def _mod_spec(d, n_lat_blocks):
    return pl.BlockSpec((None, None, N_MOD, d),
                        lambda b, r: (b, jnp.minimum(r // n_lat_blocks, 1), 0, 0))


def _qkv_kernel(x_ref, mod_ref, g_ref, wqk_ref, wvt_ref, cos_ref, sa_ref, sb_ref,
                qk_ref, vt_ref, amax_ref, *, rot, q_scale):
    h = _modulated(x_ref[...], g_ref[...], mod_ref[0:1, :], mod_ref[1:2, :]).astype(BF16)
    qk = jnp.dot(h, wqk_ref[...], preferred_element_type=F32)
    cos, sa, sb = cos_ref[...], sa_ref[...], sb_ref[...]
    n_blk = qk.shape[1] // LANES

    @pl.when(pl.program_id(1) == 0)
    def _():
        amax_ref[...] = jnp.zeros(amax_ref.shape, F32)

    for c in range(n_blk):
        blk = qk[:, c * LANES:(c + 1) * LANES]
        y = blk * cos + pltpu.roll(blk, LANES - rot, 1) * sa + pltpu.roll(blk, rot, 1) * sb
        if c < n_blk // 2:
            y = y * q_scale
        yb = y.astype(BF16)
        qk_ref[c] = yb
        col_max = jnp.max(jnp.abs(yb.astype(F32)), axis=0, keepdims=True)
        amax_ref[c] = jnp.maximum(amax_ref[c], jnp.broadcast_to(col_max, amax_ref.shape[1:]))
    vt = lax.dot_general(wvt_ref[...], h, (((1,), (1,)), ((), ())), preferred_element_type=F32)
    n_heads, dv_ext, tm = vt_ref.shape
    dv = vt.shape[0] // n_heads
    pad_row = lax.broadcasted_iota(jnp.int32, (dv_ext - dv, tm), 0)
    pad = jnp.where(pad_row == 0, 1.0, 0.0).astype(BF16)
    for hd in range(n_heads):
        vt_ref[hd, :dv, :] = vt[hd * dv:(hd + 1) * dv, :].astype(BF16)
        vt_ref[hd, dv:, :] = pad


def _qkv_call(xt, mods, g_pre, wqk, wvt, cos, sa, sb, *, n_lat, rot, dk, n_heads):
    b, t, d = xt.shape
    tm = ROW_BLOCK
    n_lat_blocks = n_lat // tm
    n_qk = wqk.shape[1] // LANES
    dv_ext = wvt.shape[0] // n_heads + V_PAD_ROWS
    return pl.pallas_call(
        functools.partial(_qkv_kernel, rot=rot, q_scale=dk ** -0.5 * math.log2(math.e)),
        grid=(b, t // tm),
        in_specs=[
            pl.BlockSpec((None, tm, d), lambda bi, r: (bi, r, 0)),
            _mod_spec(d, n_lat_blocks),
            _const_spec((1, d)),
            _const_spec(wqk.shape),
            _const_spec(wvt.shape),
            pl.BlockSpec((tm, LANES), lambda bi, r: (r, 0)),
            pl.BlockSpec((tm, LANES), lambda bi, r: (r, 0)),
            pl.BlockSpec((tm, LANES), lambda bi, r: (r, 0)),
        ],
        out_specs=[
            pl.BlockSpec((None, n_qk, tm, LANES), lambda bi, r: (bi, 0, r, 0)),
            pl.BlockSpec((None, n_heads, dv_ext, tm), lambda bi, r: (bi, 0, 0, r)),
            pl.BlockSpec((None, n_qk, SUBLANES, LANES), lambda bi, r: (bi, 0, 0, 0)),
        ],
        out_shape=[
            jax.ShapeDtypeStruct((b, n_qk, t, LANES), BF16),
            jax.ShapeDtypeStruct((b, n_heads, dv_ext, t), BF16),
            jax.ShapeDtypeStruct((b, n_qk, SUBLANES, LANES), F32),
        ],
        compiler_params=_params(2),
        name="qkv_rope",
    )(xt, mods, g_pre, wqk, wvt, cos, sa, sb)


def _pow2_below(x):
    bits = lax.bitcast_convert_type(jnp.maximum(x, MIN_AMAX), jnp.int32)
    return lax.bitcast_convert_type(((bits >> 23) - 7) << 23, F32)


def _map_scales(amax_ref, dk):
    amax = amax_ref[0:1, :]
    lane = lax.broadcasted_iota(jnp.int32, amax.shape, 1)
    s0 = _pow2_below(jnp.max(jnp.where(lane < dk, amax, 0.0), axis=1, keepdims=True))
    s1 = _pow2_below(jnp.max(jnp.where(lane >= dk, amax, 0.0), axis=1, keepdims=True))
    return s0, s1, jnp.where(lane < dk, s0, s1)


def _hi_lo(x):
    hi = x.astype(F8).astype(F32)
    return hi, x - hi


def _attn_kernel(q_ref, k_ref, vt_ref, qmax_ref, kmax_ref, lam_ref, gsub_ref, o_ref,
                 ka_ref, kb_ref, *scratch, dk, tq, first_block, n_ring_blocks, lam_init):
    lam = (jnp.exp(jnp.sum(lam_ref[0:1, :] * lam_ref[1:2, :], keepdims=True))
           - jnp.exp(jnp.sum(lam_ref[2:3, :] * lam_ref[3:4, :], keepdims=True)) + lam_init)
    dv = o_ref.shape[1]
    ks0, ks1, ks_lanes = _map_scales(kmax_ref, dk)
    qs0, qs1, qs_lanes = _map_scales(qmax_ref, dk)
    lane = lax.broadcasted_iota(jnp.int32, (1, 2 * dk), 1)
    col = lax.broadcasted_iota(jnp.int32, (1, 2 * tq), 1)
    unscale = jnp.where(col < tq, ks0 * qs0, ks1 * qs1)

    @pl.when(pl.program_id(2) == 0)
    def _():
        def split_rows(start, size):
            hi, lo = _hi_lo(k_ref[pl.ds(start, size), :].astype(F32) * (1.0 / ks_lanes))
            ka_ref[pl.ds(start, size), :] = jnp.where(lane < dk, hi, pltpu.roll(lo, dk, 1)).astype(F8)
            kb_ref[pl.ds(start, size), :] = jnp.where(lane < dk, pltpu.roll(hi, dk, 1), lo).astype(F8)
        n_keys = k_ref.shape[0]
        n_full = n_keys // K_BLOCK

        def body(j, carry):
            split_rows(pl.multiple_of(j * K_BLOCK, K_BLOCK), K_BLOCK)
            return carry
        lax.fori_loop(0, n_full, body, 0)
        if n_keys % K_BLOCK:
            split_rows(n_full * K_BLOCK, n_keys % K_BLOCK)

    class QueryBlock:
        def __init__(self, rows, qa_ref, qb_ref, m_ref, acc_ref, *bufs):
            self.rows, self.qa_ref, self.qb_ref = rows, qa_ref, qb_ref
            self.m_ref, self.acc_ref = m_ref, acc_ref
            self.first_buf = bufs[0:2]
            self.ring = tuple(bufs[i:i + 2] for i in range(2, len(bufs), 2))
            self.ahead = min(len(self.ring) - 1, n_ring_blocks)

        def start(self):
            hi, lo = _hi_lo((q_ref[self.rows, :].astype(F32) * (1.0 / qs_lanes)).T)
            self.qa_ref[...] = jnp.concatenate([hi[:dk], hi[:dk], lo[:dk], lo[:dk]], 0).astype(F8)
            self.qb_ref[...] = jnp.concatenate([hi[dk:], hi[dk:], lo[dk:], lo[dk:]], 0).astype(F8)
            self.m_ref[...] = jnp.full(self.m_ref.shape, -jnp.inf, F32)
            self.acc_ref[...] = jnp.zeros(self.acc_ref.shape, F32)
            self.scores(*first_block, self.first_buf)
            for blk in range(self.ahead):
                self.scores(blk * K_BLOCK, K_BLOCK, self.ring[blk])
            self.absorb(*first_block, self.first_buf)

        def scores(self, start, size, buf):
            s_ref, mx_ref = buf
            for km_ref, qm_ref, c0 in ((ka_ref, self.qa_ref, 0), (kb_ref, self.qb_ref, tq)):
                kk = km_ref[pl.ds(start, size), :]
                s = jnp.dot(jnp.concatenate([kk, kk], axis=1), qm_ref[...],
                            preferred_element_type=F32)
                s_ref[:, c0:c0 + tq] = s
                mx_ref[:, c0:c0 + tq] = jnp.max(s, axis=0, keepdims=True)
            mx_ref[...] = mx_ref[...] * unscale

        def absorb(self, start, size, buf):
            s_ref, mx_ref = buf
            m_old = self.m_ref[...]
            m_new = jnp.maximum(m_old, mx_ref[...])
            alpha = jnp.exp2(m_old - m_new)
            p = jnp.exp2(s_ref[...] * unscale - m_new)
            pv = jnp.dot(vt_ref[:, pl.ds(start, size)], p.astype(BF16),
                         preferred_element_type=F32)
            self.acc_ref[...] = alpha * self.acc_ref[...] + pv
            self.m_ref[...] = m_new

        def ring_step(self, base, blk, last_scored):
            n = len(self.ring)
            if blk + self.ahead <= last_scored:
                self.scores(base + (blk + self.ahead) * K_BLOCK, K_BLOCK,
                            self.ring[(blk + self.ahead) % n])
            self.absorb(base + blk * K_BLOCK, K_BLOCK, self.ring[blk % n])

        def finish(self):
            acc = self.acc_ref[:dv, :]
            l = self.acc_ref[dv:dv + 1, :]
            o = acc[:, :tq] / l[:, :tq] - lam * (acc[:, tq:] / l[:, tq:])
            o_ref[self.rows, :] = (_rms(o.T, gsub_ref[...]) * (1.0 - lam_init)).astype(BF16)

    n_sub = q_ref.shape[0] // tq
    per_sub = len(scratch) // n_sub
    blocks = [QueryBlock(pl.ds(sub * tq, tq), *scratch[sub * per_sub:(sub + 1) * per_sub])
              for sub in range(n_sub)]
    for qb in blocks:
        qb.start()
    if n_ring_blocks:
        ahead = blocks[0].ahead

        def body(j, carry):
            base = pl.multiple_of(j * (K_UNROLL * K_BLOCK), K_UNROLL * K_BLOCK)
            for u in range(K_UNROLL):
                for qb in blocks:
                    qb.ring_step(base, u, K_UNROLL + ahead)
            return carry
        n_groups = (n_ring_blocks - ahead) // K_UNROLL
        lax.fori_loop(0, n_groups, body, 0)
        for blk in range(n_groups * K_UNROLL, n_ring_blocks):
            for qb in blocks:
                qb.ring_step(0, blk, n_ring_blocks - 1)
    for qb in blocks:
        qb.finish()


def _attn_call(qk, vt, amax, lam_vecs, g_subln, *, n_heads, dk, n_lat, n_ctx, ctx_queries,
               lam_init):
    b, _, t, _ = qk.shape
    dv_ext = vt.shape[2]
    dv = dv_ext - V_PAD_ROWS
    if ctx_queries:
        tq, n_sub, n_steps, q_off, n_keys, k_off = n_ctx, 1, 1, n_lat // n_ctx, n_ctx, n_lat // n_ctx
        first_block, n_ring_blocks = (0, n_ctx), 0
    else:
        tq, n_sub, n_steps, q_off, n_keys, k_off = (
            Q_BLOCK, Q_PER_STEP, n_lat // (Q_BLOCK * Q_PER_STEP), 0, t, 0)
        first_block, n_ring_blocks = (n_lat, n_ctx), n_lat // K_BLOCK
    rows = tq * n_sub
    kern = functools.partial(_attn_kernel, dk=dk, tq=tq, first_block=first_block,
                             n_ring_blocks=n_ring_blocks, lam_init=lam_init)

    def score_bufs(n_keys):
        return [pltpu.VMEM((n_keys, 2 * tq), F32), pltpu.VMEM((1, 2 * tq), F32)]

    scratch_set = [
        pltpu.VMEM((4 * dk, tq), F8),
        pltpu.VMEM((4 * dk, tq), F8),
        pltpu.VMEM((1, 2 * tq), F32),
        pltpu.VMEM((dv_ext, 2 * tq), F32),
        *score_bufs(first_block[1]),
        *((K_RING * score_bufs(K_BLOCK)) if n_ring_blocks else []),
    ]
    return pl.pallas_call(
        kern,
        grid=(b, n_heads, n_steps),
        in_specs=[
            pl.BlockSpec((None, None, rows, LANES), lambda bi, h, i: (bi, h, q_off + i, 0)),
            pl.BlockSpec((None, None, n_keys, LANES), lambda bi, h, i: (bi, n_heads + h, k_off, 0)),
            pl.BlockSpec((None, None, dv_ext, n_keys), lambda bi, h, i: (bi, h, 0, k_off)),
            pl.BlockSpec((None, None, SUBLANES, LANES), lambda bi, h, i: (bi, h, 0, 0)),
            pl.BlockSpec((None, None, SUBLANES, LANES), lambda bi, h, i: (bi, n_heads + h, 0, 0)),
            pl.BlockSpec(lam_vecs.shape, lambda bi, h, i: (0, 0)),
            pl.BlockSpec((1, dv), lambda bi, h, i: (0, 0)),
        ],
        out_specs=pl.BlockSpec((None, rows, dv), lambda bi, h, i: (bi, i, h)),
        out_shape=jax.ShapeDtypeStruct((b, n_steps * rows, n_heads * dv), BF16),
        scratch_shapes=[
            pltpu.VMEM((n_keys, LANES), F8),
            pltpu.VMEM((n_keys, LANES), F8),
            *(n_sub * scratch_set),
        ],
        compiler_params=_params(3),
        name="diff_attn_ctx" if ctx_queries else "diff_attn",
    )(qk, qk, vt, amax, amax, lam_vecs, g_subln)


def _attn_out_ffn_kernel(o_ref, x_ref, mod_ref, gpost_ref, gfpre_ref, gfpost_ref,
                         wo_ref, win_ref, wout_ref, out_ref):
    for g in range(x_ref.shape[0] // ROW_BLOCK):
        rows = pl.ds(g * ROW_BLOCK, ROW_BLOCK)
        y = jnp.dot(o_ref[rows, :], wo_ref[...], preferred_element_type=F32)
        out_ref[rows, :] = _residual_ffn(x_ref[rows, :], y, mod_ref, gpost_ref, gfpre_ref,
                                         gfpost_ref, win_ref, wout_ref)


def _attn_out_ffn_call(o, xt, mods, g_post, g_fpre, g_fpost, wo, win, wout, *, n_lat):
    b, rows, d = o.shape[0], o.shape[1], xt.shape[2]
    tm = _row_block(rows, n_lat)
    n_lat_blocks = n_lat // tm
    return pl.pallas_call(
        _attn_out_ffn_kernel,
        grid=(b, rows // tm),
        in_specs=[
            pl.BlockSpec((None, tm, o.shape[2]), lambda bi, r: (bi, r, 0)),
            pl.BlockSpec((None, tm, d), lambda bi, r: (bi, r, 0)),
            _mod_spec(d, n_lat_blocks),
            _const_spec((1, d)), _const_spec((1, d)), _const_spec((1, d)),
            _const_spec(wo.shape), _const_spec(win.shape), _const_spec(wout.shape),
        ],
        out_specs=pl.BlockSpec((None, tm, d), lambda bi, r: (bi, r, 0)),
        out_shape=jax.ShapeDtypeStruct((b, rows, d), F32),
        compiler_params=_params(2),
        name="attn_out_ffn",
    )(o, xt, mods, g_post, g_fpre, g_fpost, wo, win, wout)


def _cmlp_ffn_kernel(x_ref, mod_ref, gpre_ref, wuv_ref, buv_ref, gsgu_ref, bsgu_ref,
                     wsp_ref, bsp_ref, wco_ref, gpost_ref, gfpre_ref, gfpost_ref,
                     win_ref, wout_ref, out_ref):
    groups, chunk, _ = wsp_ref.shape
    for rg in range(x_ref.shape[0] // ROW_BLOCK):
        rows = pl.ds(rg * ROW_BLOCK, ROW_BLOCK)
        x = x_ref[rows, :]
        h = _modulated(x, gpre_ref[...], mod_ref[0:1, :], mod_ref[1:2, :]).astype(BF16)
        z = jnp.dot(h, wuv_ref[...], preferred_element_type=F32) + buv_ref[...]
        z = 0.5 * z * (1.0 + lax.erf(z * math.sqrt(0.5)))
        w = z.shape[1] // 2
        u, v = z[:, :w], z[:, w:]
        mu = jnp.mean(v, axis=-1, keepdims=True)
        vc = v - mu
        var = jnp.mean(vc * vc, axis=-1, keepdims=True)
        vb = (vc * lax.rsqrt(var + EPS) * gsgu_ref[...] + bsgu_ref[...]).astype(BF16)

        gd = w // groups
        n_chunks = ROW_BLOCK // chunk
        mixed = []
        for g in range(groups):
            rhs = jnp.concatenate(
                [vb[n * chunk:(n + 1) * chunk, g * gd:(g + 1) * gd] for n in range(n_chunks)],
                axis=1)
            mixed.append(jnp.dot(wsp_ref[g], rhs, preferred_element_type=F32) + bsp_ref[g])
        sv = jnp.concatenate(
            [jnp.concatenate([mixed[g][:, n * gd:(n + 1) * gd] for g in range(groups)], axis=1)
             for n in range(n_chunks)], axis=0)
        y = jnp.dot((u * sv).astype(BF16), wco_ref[...], preferred_element_type=F32)
        out_ref[rows, :] = _residual_ffn(x, y, mod_ref, gpost_ref, gfpre_ref, gfpost_ref,
                                         win_ref, wout_ref)


def _cmlp_ffn_call(xt, mods, g_pre, wuv, buv, g_sgu, b_sgu, wsp, bsp, wco,
                   g_post, g_fpre, g_fpost, win, wout, *, n_lat):
    b, rows, d = xt.shape
    tm = _row_block(rows, n_lat)
    n_lat_blocks = n_lat // tm
    vec = lambda a: _const_spec((1, a.shape[-1]))
    return pl.pallas_call(
        _cmlp_ffn_kernel,
        grid=(b, rows // tm),
        in_specs=[
            pl.BlockSpec((None, tm, d), lambda bi, r: (bi, r, 0)),
            _mod_spec(d, n_lat_blocks),
            vec(g_pre), _const_spec(wuv.shape), vec(buv), vec(g_sgu), vec(b_sgu),
            _const_spec(wsp.shape), _const_spec(bsp.shape), _const_spec(wco.shape),
            vec(g_post), vec(g_fpre), vec(g_fpost),
            _const_spec(win.shape), _const_spec(wout.shape),
        ],
        out_specs=pl.BlockSpec((None, tm, d), lambda bi, r: (bi, r, 0)),
        out_shape=jax.ShapeDtypeStruct((b, rows, d), F32),
        compiler_params=_params(2),
        name="cmlp_ffn",
    )(xt, mods, g_pre, wuv, buv, g_sgu, b_sgu, wsp, bsp, wco, g_post, g_fpre, g_fpost, win, wout)


def _rope_tables(n_lat, n_ctx, dk):
    axis_dim = dk // 2
    rot = axis_dim // 2
    tok = jnp.arange(n_lat, dtype=jnp.int32)
    row = (tok // ROPE_GRID_W).astype(F32)
    col = (tok % ROPE_GRID_W).astype(F32)
    inv_freq = ROPE_BASE ** (-jnp.arange(0, axis_dim, 2, dtype=F32) / axis_dim)
    lane = jnp.arange(LANES, dtype=jnp.int32)
    in_map = lane % dk
    freq = inv_freq[in_map % rot]
    pos = jnp.where((in_map < axis_dim)[None, :], row[:, None], col[:, None])
    ang = pos * freq[None, :]
    first = ((in_map % axis_dim) < rot)[None, :]
    cos, sin = jnp.cos(ang), jnp.sin(ang)
    sa = jnp.where(first, -sin, 0.0)
    sb = jnp.where(first, 0.0, sin)
    pad = lambda a, v: jnp.concatenate([a, jnp.full((n_ctx, LANES), v, F32)], axis=0)
    return pad(cos, 1.0), pad(sa, 0.0), pad(sb, 0.0), rot


def kernel(x, c, ctx, c_ctx, w_ada, b_ada, g_mix_pre, g_mix_post, g_ffn_pre, g_ffn_post, w_qkv, w_attn_out, lam_q1, lam_k1, lam_q2, lam_k2, g_subln, w_uv, b_uv, g_sgu, b_sgu, w_spatial, b_spatial, w_cmlp_out, w_ffn_in, w_ffn_out):
    b, s, d = x.shape
    n_ctx = ctx.shape[1]
    depth = w_ada.shape[0]
    dk = lam_q1.shape[-1]
    dv = g_subln.shape[-1]
    n_heads = w_attn_out.shape[1] // dv
    qk_w = 2 * n_heads * dk * 2
    assert 2 * dk == LANES and dv == LANES
    assert K_UNROLL % K_RING == 0 and s % K_BLOCK == 0 and s % (Q_BLOCK * Q_PER_STEP) == 0 and s % n_ctx == 0
    assert s % ROW_BLOCK == 0 and n_ctx % ROW_BLOCK == 0 and b + 1 <= SUBLANES

    cond = jnp.zeros((SUBLANES, d), F32).at[:b].set(c).at[b].set(c_ctx)
    mod_all = _ada_call(cond, w_ada, b_ada)
    mod_lat = mod_all[:, :b].reshape(depth, b, 1, N_MOD, d)
    mod_ctx = jnp.broadcast_to(mod_all[:, b].reshape(depth, 1, 1, N_MOD, d), mod_lat.shape)
    mods = jnp.concatenate([mod_lat, mod_ctx], axis=2)

    cos, sa, sb, rot = _rope_tables(s, n_ctx, dk)
    last_attn = max(i for i in range(depth) if i % N_MIXERS == 0)
    row_vec = lambda a: a.reshape(1, -1)

    xt = jnp.concatenate([x, ctx], axis=1)
    for i in range(depth):
        j = i // N_MIXERS
        ctx_live = i < last_attn
        win = w_ffn_in[i].astype(BF16)
        wout = w_ffn_out[i].astype(BF16)
        tail = (row_vec(g_mix_post[i]), row_vec(g_ffn_pre[i]), row_vec(g_ffn_post[i]))
        if i % N_MIXERS == 0:
            lam_init = 0.8 - 0.6 * math.exp(-0.3 * i)
            wqk = w_qkv[j][:, :qk_w].astype(BF16)
            wvt = w_qkv[j][:, qk_w:].T.astype(BF16)
            qk, vt, amax = _qkv_call(xt, mods[i], row_vec(g_mix_pre[i]), wqk, wvt, cos, sa, sb,
                                     n_lat=s, rot=rot, dk=dk, n_heads=n_heads)
            lam_vecs = jnp.stack([lam_q1[j], lam_k1[j], lam_q2[j], lam_k2[j]]).astype(F32)
            attend = functools.partial(_attn_call, qk, vt, amax, lam_vecs, row_vec(g_subln[j]),
                                       n_heads=n_heads, dk=dk, n_lat=s, n_ctx=n_ctx,
                                       lam_init=lam_init)
            o = attend(ctx_queries=False)
            if ctx_live:
                o = jnp.concatenate([o, attend(ctx_queries=True)], axis=1)
            xt = _attn_out_ffn_call(o, xt, mods[i], *tail, w_attn_out[j].astype(BF16), win, wout,
                                    n_lat=s)
        else:
            if not ctx_live and xt.shape[1] != s:
                xt = xt[:, :s]
            xt = _cmlp_ffn_call(
                xt, mods[i], row_vec(g_mix_pre[i]), w_uv[j].astype(BF16), row_vec(b_uv[j]),
                row_vec(g_sgu[j]), row_vec(b_sgu[j]), w_spatial[j].astype(BF16),
                b_spatial[j][:, :, None], w_cmlp_out[j].astype(BF16), *tail, win, wout,
                n_lat=s)
    return xt[:, :s] if xt.shape[1] != s else xt
```

```python
import functools
import math

import jax
import jax.numpy as jnp
import numpy as np
from jax import lax
from jax.experimental import pallas as pl
from jax.experimental.pallas import tpu as pltpu

F32 = jnp.float32
BF16 = jnp.bfloat16

EPS = 1e-6
ROPE_GRID_W = 64
ROPE_BASE = 10000.0
N_MIXERS = 2
N_MOD = 6

LANES = 128
SUBLANES = 8
VMEM_LIMIT_BYTES = 56 << 20

ROW_BLOCK = 256
ROW_GROUPS = 2
Q_BLOCK = 256
Q_PER_STEP = 2
K_BLOCK = 1024
V_PAD_ROWS = 16
K_RING = 3
K_UNROLL = 6


def _const_spec(shape):
    return pl.BlockSpec(shape, lambda *_: (0,) * len(shape), pipeline_mode=pl.Buffered(1))


def _params(n_grid, flags=None):
    return pltpu.CompilerParams(
        dimension_semantics=("arbitrary",) * n_grid, vmem_limit_bytes=VMEM_LIMIT_BYTES,
        flags=flags)


def _rms(x, g):
    return x * lax.rsqrt(jnp.mean(x * x, axis=-1, keepdims=True) + EPS) * g


def _modulated(x, g, shift, scale):
    return _rms(x, g) * (1.0 + scale) + shift


def _ada_kernel(c_ref, w_ref, b_ref, o_ref):
    s = jax.nn.silu(c_ref[...])
    o_ref[...] = jnp.dot(s, w_ref[...], preferred_element_type=F32,
                         precision=lax.Precision.HIGHEST) + b_ref[...]


def _ada_call(cond, w_ada, b_ada):
    depth, d, n = w_ada.shape
    tn = 2048
    return pl.pallas_call(
        _ada_kernel,
        grid=(depth, n // tn),
        in_specs=[
            pl.BlockSpec(cond.shape, lambda l, k: (0, 0)),
            pl.BlockSpec((None, d, tn), lambda l, k: (l, 0, k)),
            pl.BlockSpec((None, 1, tn), lambda l, k: (l, 0, k)),
        ],
        out_specs=pl.BlockSpec((None, cond.shape[0], tn), lambda l, k: (l, 0, k)),
        out_shape=jax.ShapeDtypeStruct((depth, cond.shape[0], n), F32),
        compiler_params=_params(2),
        name="ada_mod",
    )(cond, w_ada, b_ada.reshape(depth, 1, n))


def _residual_ffn(x, y, mod_ref, gpost_ref, gfpre_ref, gfpost_ref, win_ref, wout_ref):
    x = x + mod_ref[2:3, :] * _rms(y, gpost_ref[...])
    h = _modulated(x, gfpre_ref[...], mod_ref[3:4, :], mod_ref[4:5, :]).astype(BF16)
    gu = jnp.dot(h, win_ref[...], preferred_element_type=F32)
    f = wout_ref.shape[0]
    a = (jax.nn.silu(gu[:, :f]) * gu[:, f:]).astype(BF16)
    y2 = jnp.dot(a, wout_ref[...], preferred_element_type=F32)
    return x + mod_ref[5:6, :] * _rms(y2, gfpost_ref[...])


def _row_block(rows, n_lat):
    wide = ROW_BLOCK * ROW_GROUPS
    return wide if rows == n_lat and rows % wide == 0 else ROW_BLOCK


def _mod_spec(d, n_lat_blocks):
    return pl.BlockSpec((None, None, N_MOD, d),
                        lambda b, r: (b, jnp.minimum(r // n_lat_blocks, 1), 0, 0))


def _stream_specs(streams, tm, n_lat_blocks):
    d = streams[0].shape[2]
    if len(streams) == 1:
        return [pl.BlockSpec((None, tm, d), lambda bi, r: (bi, r, 0))]
    return [pl.BlockSpec((None, tm, d), lambda bi, r: (bi, jnp.minimum(r, n_lat_blocks - 1), 0)),
            pl.BlockSpec((None, tm, d), lambda bi, r: (bi, jnp.maximum(r - n_lat_blocks, 0), 0))]


def _token_rows(refs, n_rest, n_lat_blocks):
    stream_refs, rest = refs[:len(refs) - n_rest], refs[len(refs) - n_rest:]
    if len(stream_refs) == 1:
        return stream_refs[0][...], rest
    lat_ref, ctx_ref = stream_refs
    return jnp.where(pl.program_id(1) >= n_lat_blocks, ctx_ref[...], lat_ref[...]), rest


def _qkv_kernel(*refs, rot, q_scale, n_lat_blocks):
    x, (mod_ref, g_ref, wqk_ref, wvt_ref, cos_ref, sa_ref, sb_ref, qk_ref, vt_ref) = (
        _token_rows(refs, 9, n_lat_blocks))
    h = _modulated(x, g_ref[...], mod_ref[0:1, :], mod_ref[1:2, :]).astype(BF16)
    qk = jnp.dot(h, wqk_ref[...], preferred_element_type=F32)
    cos, sa, sb = cos_ref[...], sa_ref[...], sb_ref[...]
    n_blk = qk.shape[1] // LANES
    for c in range(n_blk):
        blk = qk[:, c * LANES:(c + 1) * LANES]
        y = blk * cos + pltpu.roll(blk, LANES - rot, 1) * sa + pltpu.roll(blk, rot, 1) * sb
        if c < n_blk // 2:
            y = y * q_scale
        qk_ref[c] = y.astype(BF16)
    vt = lax.dot_general(wvt_ref[...], h, (((1,), (1,)), ((), ())), preferred_element_type=F32)
    n_heads, dv_ext, tm = vt_ref.shape
    dv = vt.shape[0] // n_heads
    pad_row = lax.broadcasted_iota(jnp.int32, (dv_ext - dv, tm), 0)
    pad = jnp.where(pad_row == 0, 1.0, 0.0).astype(BF16)
    for hd in range(n_heads):
        vt_ref[hd, :dv, :] = vt[hd * dv:(hd + 1) * dv, :].astype(BF16)
        vt_ref[hd, dv:, :] = pad


def _qkv_call(streams, mods, g_pre, wqk, wvt, cos, sa, sb, *, n_lat, rot, dk, n_heads):
    b, d = streams[0].shape[0], streams[0].shape[2]
    t = sum(a.shape[1] for a in streams)
    tm = ROW_BLOCK
    n_lat_blocks = n_lat // tm
    n_qk = wqk.shape[1] // LANES
    dv_ext = wvt.shape[0] // n_heads + V_PAD_ROWS
    return pl.pallas_call(
        functools.partial(_qkv_kernel, rot=rot, q_scale=dk ** -0.5 * math.log2(math.e),
                          n_lat_blocks=n_lat_blocks),
        grid=(b, t // tm),
        in_specs=[
            *_stream_specs(streams, tm, n_lat_blocks),
            _mod_spec(d, n_lat_blocks),
            _const_spec((1, d)),
            _const_spec(wqk.shape),
            _const_spec(wvt.shape),
            pl.BlockSpec((tm, LANES), lambda bi, r: (r, 0)),
            pl.BlockSpec((tm, LANES), lambda bi, r: (r, 0)),
            pl.BlockSpec((tm, LANES), lambda bi, r: (r, 0)),
        ],
        out_specs=[
            pl.BlockSpec((None, n_qk, tm, LANES), lambda bi, r: (bi, 0, r, 0)),
            pl.BlockSpec((None, n_heads, dv_ext, tm), lambda bi, r: (bi, 0, 0, r)),
        ],
        out_shape=[
            jax.ShapeDtypeStruct((b, n_qk, t, LANES), BF16),
            jax.ShapeDtypeStruct((b, n_heads, dv_ext, t), BF16),
        ],
        compiler_params=_params(2),
        name="qkv_rope",
    )(*streams, mods, g_pre, wqk, wvt, cos, sa, sb)


def _attn_kernel(q_ref, k_ref, vt_ref, lam_ref, gsub_ref, o_ref, *scratch,
                 dk, tq, first_block, n_ring_blocks, lam_init):
    lam = (jnp.exp(jnp.sum(lam_ref[0:1, :] * lam_ref[1:2, :], keepdims=True))
           - jnp.exp(jnp.sum(lam_ref[2:3, :] * lam_ref[3:4, :], keepdims=True)) + lam_init)
    dv = o_ref.shape[1]

    class QueryBlock:
        def __init__(self, rows, qst_ref, m_ref, acc_ref, *bufs):
            self.rows, self.qst_ref, self.m_ref, self.acc_ref = rows, qst_ref, m_ref, acc_ref
            self.first_buf = bufs[0:2]
            self.ring = tuple(bufs[i:i + 2] for i in range(2, len(bufs), 2))
            self.ahead = min(len(self.ring) - 1, n_ring_blocks)

        def start(self):
            qt = q_ref[self.rows, :].astype(F32).T
            row = lax.broadcasted_iota(jnp.int32, qt.shape, 0)
            self.qst_ref[:, :tq] = jnp.where(row < dk, qt, 0.0).astype(BF16)
            self.qst_ref[:, tq:] = jnp.where(row >= dk, qt, 0.0).astype(BF16)
            self.m_ref[...] = jnp.full(self.m_ref.shape, -jnp.inf, F32)
            self.acc_ref[...] = jnp.zeros(self.acc_ref.shape, F32)
            self.scores(*first_block, self.first_buf)
            for blk in range(self.ahead):
                self.scores(blk * K_BLOCK, K_BLOCK, self.ring[blk])
            self.absorb(*first_block, self.first_buf)

        def scores(self, start, size, buf):
            s_ref, mx_ref = buf
            s = jnp.dot(k_ref[pl.ds(start, size), :], self.qst_ref[...],
                        preferred_element_type=F32)
            s_ref[...] = s
            mx_ref[...] = jnp.max(s, axis=0, keepdims=True)

        def absorb(self, start, size, buf):
            s_ref, mx_ref = buf
            m_old = self.m_ref[...]
            m_new = jnp.maximum(m_old, mx_ref[...])
            alpha = jnp.exp2(m_old - m_new)
            p = jnp.exp2(s_ref[...] - m_new)
            pv = jnp.dot(vt_ref[:, pl.ds(start, size)], p.astype(BF16),
                         preferred_element_type=F32)
            self.acc_ref[...] = alpha * self.acc_ref[...] + pv
            self.m_ref[...] = m_new

        def ring_step(self, base, blk, last_scored):
            n = len(self.ring)
            if blk + self.ahead <= last_scored:
                self.scores(base + (blk + self.ahead) * K_BLOCK, K_BLOCK,
                            self.ring[(blk + self.ahead) % n])
            self.absorb(base + blk * K_BLOCK, K_BLOCK, self.ring[blk % n])

        def finish(self):
            acc = self.acc_ref[:dv, :]
            l = self.acc_ref[dv:dv + 1, :]
            o = acc[:, :tq] / l[:, :tq] - lam * (acc[:, tq:] / l[:, tq:])
            o_ref[self.rows, :] = (_rms(o.T, gsub_ref[...]) * (1.0 - lam_init)).astype(BF16)

    n_sub = q_ref.shape[0] // tq
    per_sub = len(scratch) // n_sub
    blocks = [QueryBlock(pl.ds(sub * tq, tq), *scratch[sub * per_sub:(sub + 1) * per_sub])
              for sub in range(n_sub)]
    for qb in blocks:
        qb.start()
    if n_ring_blocks:
        ahead = blocks[0].ahead

        def body(j, carry):
            base = pl.multiple_of(j * (K_UNROLL * K_BLOCK), K_UNROLL * K_BLOCK)
            for u in range(K_UNROLL):
                for qb in blocks:
                    qb.ring_step(base, u, K_UNROLL + ahead)
            return carry
        n_groups = (n_ring_blocks - ahead) // K_UNROLL
        lax.fori_loop(0, n_groups, body, 0)
        for blk in range(n_groups * K_UNROLL, n_ring_blocks):
            for qb in blocks:
                qb.ring_step(0, blk, n_ring_blocks - 1)
    for qb in blocks:
        qb.finish()


def _attn_call(qk, vt, lam_vecs, g_subln, *, n_heads, dk, n_lat, n_ctx, ctx_queries, lam_init):
    b, _, t, _ = qk.shape
    dv_ext = vt.shape[2]
    dv = dv_ext - V_PAD_ROWS
    if ctx_queries:
        tq, n_sub, n_steps, q_off, n_keys, k_off = n_ctx, 1, 1, n_lat // n_ctx, n_ctx, n_lat // n_ctx
        first_block, n_ring_blocks = (0, n_ctx), 0
    else:
        tq, n_sub, n_steps, q_off, n_keys, k_off = (
            Q_BLOCK, Q_PER_STEP, n_lat // (Q_BLOCK * Q_PER_STEP), 0, t, 0)
        first_block, n_ring_blocks = (n_lat, n_ctx), n_lat // K_BLOCK
    rows = tq * n_sub
    kern = functools.partial(_attn_kernel, dk=dk, tq=tq, first_block=first_block,
                             n_ring_blocks=n_ring_blocks, lam_init=lam_init)

    def score_bufs(n_keys):
        return [pltpu.VMEM((n_keys, 2 * tq), F32), pltpu.VMEM((1, 2 * tq), F32)]

    scratch_set = [
        pltpu.VMEM((2 * dk, 2 * tq), BF16),
        pltpu.VMEM((1, 2 * tq), F32),
        pltpu.VMEM((dv_ext, 2 * tq), F32),
        *score_bufs(first_block[1]),
        *((K_RING * score_bufs(K_BLOCK)) if n_ring_blocks else []),
    ]
    return pl.pallas_call(
        kern,
        grid=(b, n_heads, n_steps),
        in_specs=[
            pl.BlockSpec((None, None, rows, LANES), lambda bi, h, i: (bi, h, q_off + i, 0)),
            pl.BlockSpec((None, None, n_keys, LANES), lambda bi, h, i: (bi, n_heads + h, k_off, 0)),
            pl.BlockSpec((None, None, dv_ext, n_keys), lambda bi, h, i: (bi, h, 0, k_off)),
            pl.BlockSpec(lam_vecs.shape, lambda bi, h, i: (0, 0)),
            pl.BlockSpec((1, dv), lambda bi, h, i: (0, 0)),
        ],
        out_specs=pl.BlockSpec((None, rows, dv), lambda bi, h, i: (bi, i, h)),
        out_shape=jax.ShapeDtypeStruct((b, n_steps * rows, n_heads * dv), BF16),
        scratch_shapes=n_sub * scratch_set,
        compiler_params=_params(3),
        name="diff_attn_ctx" if ctx_queries else "diff_attn",
    )(qk, qk, vt, lam_vecs, g_subln)


def _attn_out_ffn_kernel(o_ref, *refs, n_lat_blocks):
    x, (mod_ref, gpost_ref, gfpre_ref, gfpost_ref, wo_ref, win_ref, wout_ref, out_ref) = (
        _token_rows(refs, 8, n_lat_blocks))
    for g in range(out_ref.shape[0] // ROW_BLOCK):
        rows = slice(g * ROW_BLOCK, (g + 1) * ROW_BLOCK)
        y = jnp.dot(o_ref[rows, :], wo_ref[...], preferred_element_type=F32)
        out_ref[rows, :] = _residual_ffn(x[rows, :], y, mod_ref, gpost_ref, gfpre_ref,
                                         gfpost_ref, win_ref, wout_ref)


def _attn_out_ffn_call(o, streams, mods, g_post, g_fpre, g_fpost, wo, win, wout, *, n_lat):
    b, rows, d = o.shape[0], o.shape[1], streams[0].shape[2]
    tm = _row_block(rows, n_lat)
    n_lat_blocks = n_lat // tm
    return pl.pallas_call(
        functools.partial(_attn_out_ffn_kernel, n_lat_blocks=n_lat_blocks),
        grid=(b, rows // tm),
        in_specs=[
            pl.BlockSpec((None, tm, o.shape[2]), lambda bi, r: (bi, r, 0)),
            *_stream_specs(streams, tm, n_lat_blocks),
            _mod_spec(d, n_lat_blocks),
            _const_spec((1, d)), _const_spec((1, d)), _const_spec((1, d)),
            _const_spec(wo.shape), _const_spec(win.shape), _const_spec(wout.shape),
        ],
        out_specs=pl.BlockSpec((None, tm, d), lambda bi, r: (bi, r, 0)),
        out_shape=jax.ShapeDtypeStruct((b, rows, d), F32),
        compiler_params=_params(2),
        name="attn_out_ffn",
    )(o, *streams, mods, g_post, g_fpre, g_fpost, wo, win, wout)


def _cmlp_ffn_kernel(x_ref, mod_ref, gpre_ref, wuv_ref, buv_ref, gsgu_ref, bsgu_ref,
                     wsp_ref, bsp_ref, wco_ref, gpost_ref, gfpre_ref, gfpost_ref,
                     win_ref, wout_ref, out_ref):
    groups, chunk, _ = wsp_ref.shape
    for rg in range(x_ref.shape[0] // ROW_BLOCK):
        rows = pl.ds(rg * ROW_BLOCK, ROW_BLOCK)
        x = x_ref[rows, :]
        h = _modulated(x, gpre_ref[...], mod_ref[0:1, :], mod_ref[1:2, :]).astype(BF16)
        z = jnp.dot(h, wuv_ref[...], preferred_element_type=F32) + buv_ref[...]
        z = 0.5 * z * (1.0 + lax.erf(z * math.sqrt(0.5)))
        w = z.shape[1] // 2
        u, v = z[:, :w], z[:, w:]
        mu = jnp.mean(v, axis=-1, keepdims=True)
        vc = v - mu
        var = jnp.mean(vc * vc, axis=-1, keepdims=True)
        vb = (vc * lax.rsqrt(var + EPS) * gsgu_ref[...] + bsgu_ref[...]).astype(BF16)

        gd = w // groups
        n_chunks = ROW_BLOCK // chunk
        mixed = []
        for g in range(groups):
            rhs = jnp.concatenate(
                [vb[n * chunk:(n + 1) * chunk, g * gd:(g + 1) * gd] for n in range(n_chunks)],
                axis=1)
            mixed.append(jnp.dot(wsp_ref[g], rhs, preferred_element_type=F32) + bsp_ref[g])
        sv = jnp.concatenate(
            [jnp.concatenate([mixed[g][:, n * gd:(n + 1) * gd] for g in range(groups)], axis=1)
             for n in range(n_chunks)], axis=0)
        y = jnp.dot((u * sv).astype(BF16), wco_ref[...], preferred_element_type=F32)
        out_ref[rows, :] = _residual_ffn(x, y, mod_ref, gpost_ref, gfpre_ref, gfpost_ref,
                                         win_ref, wout_ref)


def _cmlp_ffn_call(xt, mods, g_pre, wuv, buv, g_sgu, b_sgu, wsp, bsp, wco,
                   g_post, g_fpre, g_fpost, win, wout, *, n_lat):
    b, rows, d = xt.shape
    tm = _row_block(rows, n_lat)
    n_lat_blocks = n_lat // tm
    vec = lambda a: _const_spec((1, a.shape[-1]))
    return pl.pallas_call(
        _cmlp_ffn_kernel,
        grid=(b, rows // tm),
        in_specs=[
            pl.BlockSpec((None, tm, d), lambda bi, r: (bi, r, 0)),
            _mod_spec(d, n_lat_blocks),
            vec(g_pre), _const_spec(wuv.shape), vec(buv), vec(g_sgu), vec(b_sgu),
            _const_spec(wsp.shape), _const_spec(bsp.shape), _const_spec(wco.shape),
            vec(g_post), vec(g_fpre), vec(g_fpost),
            _const_spec(win.shape), _const_spec(wout.shape),
        ],
        out_specs=pl.BlockSpec((None, tm, d), lambda bi, r: (bi, r, 0)),
        out_shape=jax.ShapeDtypeStruct((b, rows, d), F32),
        compiler_params=_params(2),
        name="cmlp_ffn",
    )(xt, mods, g_pre, wuv, buv, g_sgu, b_sgu, wsp, bsp, wco, g_post, g_fpre, g_fpost, win, wout)


def _rope_tables(n_lat, n_ctx, dk):
    axis_dim = dk // 2
    rot = axis_dim // 2
    tok = np.arange(n_lat)
    row = (tok // ROPE_GRID_W).astype(np.float32)
    col = (tok % ROPE_GRID_W).astype(np.float32)
    inv_freq = (np.float32(ROPE_BASE)
                ** (-np.arange(0, axis_dim, 2, dtype=np.float32) / np.float32(axis_dim)))
    in_map = np.arange(LANES) % dk
    freq = inv_freq[in_map % rot].astype(np.float32)
    pos = np.where((in_map < axis_dim)[None, :], row[:, None], col[:, None])
    ang = (pos * freq[None, :]).astype(np.float32)
    first = ((in_map % axis_dim) < rot)[None, :]
    cos, sin = np.cos(ang), np.sin(ang)
    sa = np.where(first, -sin, 0.0)
    sb = np.where(first, 0.0, sin)
    pad = lambda a, v: jnp.asarray(np.concatenate(
        [a, np.full((n_ctx, LANES), v)], axis=0).astype(np.float32))
    return pad(cos, 1.0), pad(sa, 0.0), pad(sb, 0.0), rot


def kernel(x, c, ctx, c_ctx, w_ada, b_ada, g_mix_pre, g_mix_post, g_ffn_pre, g_ffn_post, w_qkv, w_attn_out, lam_q1, lam_k1, lam_q2, lam_k2, g_subln, w_uv, b_uv, g_sgu, b_sgu, w_spatial, b_spatial, w_cmlp_out, w_ffn_in, w_ffn_out):
    b, s, d = x.shape
    n_ctx = ctx.shape[1]
    depth = w_ada.shape[0]
    dk = lam_q1.shape[-1]
    dv = g_subln.shape[-1]
    n_heads = w_attn_out.shape[1] // dv
    qk_w = 2 * n_heads * dk * 2
    assert 2 * dk == LANES and dv == LANES
    assert K_UNROLL % K_RING == 0 and s % K_BLOCK == 0 and s % (Q_BLOCK * Q_PER_STEP) == 0 and s % n_ctx == 0
    assert s % ROW_BLOCK == 0 and n_ctx % ROW_BLOCK == 0 and b + 1 <= SUBLANES

    cond = jnp.zeros((SUBLANES, d), F32).at[:b].set(c).at[b].set(c_ctx)
    mod_all = _ada_call(cond, w_ada, b_ada)
    mod_lat = mod_all[:, :b].reshape(depth, b, 1, N_MOD, d)
    mod_ctx = jnp.broadcast_to(mod_all[:, b].reshape(depth, 1, 1, N_MOD, d), mod_lat.shape)
    mods = jnp.concatenate([mod_lat, mod_ctx], axis=2)

    cos, sa, sb, rot = _rope_tables(s, n_ctx, dk)
    last_attn = max(i for i in range(depth) if i % N_MIXERS == 0)
    row_vec = lambda a: a.reshape(1, -1)

    streams = (x, ctx)
    for i in range(depth):
        j = i // N_MIXERS
        ctx_live = i < last_attn
        win = w_ffn_in[i].astype(BF16)
        wout = w_ffn_out[i].astype(BF16)
        tail = (row_vec(g_mix_post[i]), row_vec(g_ffn_pre[i]), row_vec(g_ffn_post[i]))
        if i % N_MIXERS == 0:
            lam_init = 0.8 - 0.6 * math.exp(-0.3 * i)
            wqk = w_qkv[j][:, :qk_w].astype(BF16)
            wvt = w_qkv[j][:, qk_w:].T.astype(BF16)
            qk, vt = _qkv_call(streams, mods[i], row_vec(g_mix_pre[i]), wqk, wvt, cos, sa, sb,
                               n_lat=s, rot=rot, dk=dk, n_heads=n_heads)
            lam_vecs = jnp.stack([lam_q1[j], lam_k1[j], lam_q2[j], lam_k2[j]]).astype(F32)
            attend = functools.partial(_attn_call, qk, vt, lam_vecs, row_vec(g_subln[j]),
                                       n_heads=n_heads, dk=dk, n_lat=s, n_ctx=n_ctx,
                                       lam_init=lam_init)
            o = attend(ctx_queries=False)
            if ctx_live:
                o = jnp.concatenate([o, attend(ctx_queries=True)], axis=1)
            streams = (_attn_out_ffn_call(o, streams, mods[i], *tail, w_attn_out[j].astype(BF16),
                                          win, wout, n_lat=s),)
        else:
            xt, = streams
            if not ctx_live and xt.shape[1] != s:
                xt = xt[:, :s]
            streams = (_cmlp_ffn_call(
                xt, mods[i], row_vec(g_mix_pre[i]), w_uv[j].astype(BF16), row_vec(b_uv[j]),
                row_vec(g_sgu[j]), row_vec(b_sgu[j]), w_spatial[j].astype(BF16),
                b_spatial[j][:, :, None], w_cmlp_out[j].astype(BF16), *tail, win, wout,
                n_lat=s),)
    xt, = streams
    return xt[:, :s] if xt.shape[1] != s else xt
```

```python
import functools
import math

import jax
import jax.numpy as jnp
import numpy as np
from jax import lax
from jax.experimental import pallas as pl
from jax.experimental.pallas import tpu as pltpu

F32 = jnp.float32
BF16 = jnp.bfloat16

EPS = 1e-6
ROPE_GRID_W = 64
ROPE_BASE = 10000.0
N_MIXERS = 2
N_MOD = 6

LANES = 128
SUBLANES = 8
VMEM_LIMIT_BYTES = 56 << 20

ROW_BLOCK = 256
ROW_GROUPS = 2
Q_BLOCK = 256
Q_PER_STEP = 4
K_BLOCK = 1024
V_PAD_ROWS = 16
K_RING = 3
K_UNROLL = 6


def _const_spec(shape):
    return pl.BlockSpec(shape, lambda *_: (0,) * len(shape), pipeline_mode=pl.Buffered(1))


def _params(n_grid, flags=None):
    return pltpu.CompilerParams(
        dimension_semantics=("arbitrary",) * n_grid, vmem_limit_bytes=VMEM_LIMIT_BYTES,
        flags=flags)


def _rms(x, g):
    return x * lax.rsqrt(jnp.mean(x * x, axis=-1, keepdims=True) + EPS) * g


def _modulated(x, g, shift, scale):
    return _rms(x, g) * (1.0 + scale) + shift


def _ada_kernel(c_ref, w_ref, b_ref, o_ref):
    s = jax.nn.silu(c_ref[...])
    o_ref[...] = jnp.dot(s, w_ref[...], preferred_element_type=F32,
                         precision=lax.Precision.HIGHEST) + b_ref[...]


def _ada_call(cond, w_ada, b_ada):
    depth, d, n = w_ada.shape
    tn = 2048
    return pl.pallas_call(
        _ada_kernel,
        grid=(depth, n // tn),
        in_specs=[
            pl.BlockSpec(cond.shape, lambda l, k: (0, 0)),
            pl.BlockSpec((None, d, tn), lambda l, k: (l, 0, k)),
            pl.BlockSpec((None, 1, tn), lambda l, k: (l, 0, k)),
        ],
        out_specs=pl.BlockSpec((None, cond.shape[0], tn), lambda l, k: (l, 0, k)),
        out_shape=jax.ShapeDtypeStruct((depth, cond.shape[0], n), F32),
        compiler_params=_params(2),
        name="ada_mod",
    )(cond, w_ada, b_ada.reshape(depth, 1, n))


def _residual_ffn(x, y, mod_ref, gpost_ref, gfpre_ref, gfpost_ref, win_ref, wout_ref):
    x = x + mod_ref[2:3, :] * _rms(y, gpost_ref[...])
    h = _modulated(x, gfpre_ref[...], mod_ref[3:4, :], mod_ref[4:5, :]).astype(BF16)
    gu = jnp.dot(h, win_ref[...], preferred_element_type=F32)
    f = wout_ref.shape[0]
    a = (jax.nn.silu(gu[:, :f]) * gu[:, f:]).astype(BF16)
    y2 = jnp.dot(a, wout_ref[...], preferred_element_type=F32)
    return x + mod_ref[5:6, :] * _rms(y2, gfpost_ref[...])


def _row_block(rows, n_lat):
    wide = ROW_BLOCK * ROW_GROUPS
    return wide if rows == n_lat and rows % wide == 0 else ROW_BLOCK


def _mod_spec(d, n_lat_blocks):
    return pl.BlockSpec((None, None, N_MOD, d),
                        lambda b, r: (b, jnp.minimum(r // n_lat_blocks, 1), 0, 0))


def _stream_specs(streams, tm, n_lat_blocks):
    d = streams[0].shape[2]
    if len(streams) == 1:
        return [pl.BlockSpec((None, tm, d), lambda bi, r: (bi, r, 0))]
    return [pl.BlockSpec((None, tm, d), lambda bi, r: (bi, jnp.minimum(r, n_lat_blocks - 1), 0)),
            pl.BlockSpec((None, tm, d), lambda bi, r: (bi, jnp.maximum(r - n_lat_blocks, 0), 0))]


def _token_rows(refs, n_rest, n_lat_blocks):
    stream_refs, rest = refs[:len(refs) - n_rest], refs[len(refs) - n_rest:]
    if len(stream_refs) == 1:
        return stream_refs[0][...], rest
    lat_ref, ctx_ref = stream_refs
    return jnp.where(pl.program_id(1) >= n_lat_blocks, ctx_ref[...], lat_ref[...]), rest


def _qkv_kernel(*refs, rot, q_scale, n_lat_blocks):
    x, (mod_ref, g_ref, wqk_ref, wvt_ref, cos_ref, sa_ref, sb_ref, qk_ref, vt_ref) = (
        _token_rows(refs, 9, n_lat_blocks))
    h = _modulated(x, g_ref[...], mod_ref[0:1, :], mod_ref[1:2, :]).astype(BF16)
    qk = jnp.dot(h, wqk_ref[...], preferred_element_type=F32)
    cos, sa, sb = cos_ref[...], sa_ref[...], sb_ref[...]
    n_blk = qk.shape[1] // LANES
    for c in range(n_blk):
        blk = qk[:, c * LANES:(c + 1) * LANES]
        y = blk * cos + pltpu.roll(blk, LANES - rot, 1) * sa + pltpu.roll(blk, rot, 1) * sb
        if c < n_blk // 2:
            y = y * q_scale
        qk_ref[c] = y.astype(BF16)
    vt = lax.dot_general(wvt_ref[...], h, (((1,), (1,)), ((), ())), preferred_element_type=F32)
    n_heads, dv_ext, tm = vt_ref.shape
    dv = vt.shape[0] // n_heads
    pad_row = lax.broadcasted_iota(jnp.int32, (dv_ext - dv, tm), 0)
    pad = jnp.where(pad_row == 0, 1.0, 0.0).astype(BF16)
    for hd in range(n_heads):
        vt_ref[hd, :dv, :] = vt[hd * dv:(hd + 1) * dv, :].astype(BF16)
        vt_ref[hd, dv:, :] = pad


def _qkv_call(streams, mods, g_pre, wqk, wvt, cos, sa, sb, *, n_lat, rot, dk, n_heads):
    b, d = streams[0].shape[0], streams[0].shape[2]
    t = sum(a.shape[1] for a in streams)
    tm = ROW_BLOCK
    n_lat_blocks = n_lat // tm
    n_qk = wqk.shape[1] // LANES
    dv_ext = wvt.shape[0] // n_heads + V_PAD_ROWS
    return pl.pallas_call(
        functools.partial(_qkv_kernel, rot=rot, q_scale=dk ** -0.5 * math.log2(math.e),
                          n_lat_blocks=n_lat_blocks),
        grid=(b, t // tm),
        in_specs=[
            *_stream_specs(streams, tm, n_lat_blocks),
            _mod_spec(d, n_lat_blocks),
            _const_spec((1, d)),
            _const_spec(wqk.shape),
            _const_spec(wvt.shape),
            pl.BlockSpec((tm, LANES), lambda bi, r: (r, 0)),
            pl.BlockSpec((tm, LANES), lambda bi, r: (r, 0)),
            pl.BlockSpec((tm, LANES), lambda bi, r: (r, 0)),
        ],
        out_specs=[
            pl.BlockSpec((None, n_qk, tm, LANES), lambda bi, r: (bi, 0, r, 0)),
            pl.BlockSpec((None, n_heads, dv_ext, tm), lambda bi, r: (bi, 0, 0, r)),
        ],
        out_shape=[
            jax.ShapeDtypeStruct((b, n_qk, t, LANES), BF16),
            jax.ShapeDtypeStruct((b, n_heads, dv_ext, t), BF16),
        ],
        compiler_params=_params(2),
        name="qkv_rope",
    )(*streams, mods, g_pre, wqk, wvt, cos, sa, sb)


def _attn_kernel(q_ref, k_ref, vt_ref, lam_ref, gsub_ref, o_ref, *scratch,
                 dk, tq, first_block, n_ring_blocks, lam_init):
    lam = (jnp.exp(jnp.sum(lam_ref[0:1, :] * lam_ref[1:2, :], keepdims=True))
           - jnp.exp(jnp.sum(lam_ref[2:3, :] * lam_ref[3:4, :], keepdims=True)) + lam_init)
    dv = o_ref.shape[1]

    class QueryBlock:
        def __init__(self, rows, qst_ref, m_ref, acc_ref, *bufs):
            self.rows, self.qst_ref, self.m_ref, self.acc_ref = rows, qst_ref, m_ref, acc_ref
            self.first_buf = bufs[0:2]
            self.ring = tuple(bufs[i:i + 2] for i in range(2, len(bufs), 2))
            self.ahead = min(len(self.ring) - 1, n_ring_blocks)

        def start(self):
            qt = q_ref[self.rows, :].astype(F32).T
            row = lax.broadcasted_iota(jnp.int32, qt.shape, 0)
            self.qst_ref[:, :tq] = jnp.where(row < dk, qt, 0.0).astype(BF16)
            self.qst_ref[:, tq:] = jnp.where(row >= dk, qt, 0.0).astype(BF16)
            self.m_ref[...] = jnp.full(self.m_ref.shape, -jnp.inf, F32)
            self.acc_ref[...] = jnp.zeros(self.acc_ref.shape, F32)
            self.scores(*first_block, self.first_buf)
            for blk in range(self.ahead):
                self.scores(blk * K_BLOCK, K_BLOCK, self.ring[blk])
            self.absorb(*first_block, self.first_buf)

        def scores(self, start, size, buf):
            s_ref, mx_ref = buf
            s = jnp.dot(k_ref[pl.ds(start, size), :], self.qst_ref[...],
                        preferred_element_type=F32)
            s_ref[...] = s
            mx_ref[...] = jnp.max(s, axis=0, keepdims=True)

        def absorb(self, start, size, buf):
            s_ref, mx_ref = buf
            m_old = self.m_ref[...]
            m_new = jnp.maximum(m_old, mx_ref[...])
            alpha = jnp.exp2(m_old - m_new)
            p = jnp.exp2(s_ref[...] - m_new)
            pv = jnp.dot(vt_ref[:, pl.ds(start, size)], p.astype(BF16),
                         preferred_element_type=F32)
            self.acc_ref[...] = alpha * self.acc_ref[...] + pv
            self.m_ref[...] = m_new

        def ring_step(self, base, blk, last_scored):
            n = len(self.ring)
            if blk + self.ahead <= last_scored:
                self.scores(base + (blk + self.ahead) * K_BLOCK, K_BLOCK,
                            self.ring[(blk + self.ahead) % n])
            self.absorb(base + blk * K_BLOCK, K_BLOCK, self.ring[blk % n])

        def finish(self):
            acc = self.acc_ref[:dv, :]
            l = self.acc_ref[dv:dv + 1, :]
            o = acc[:, :tq] / l[:, :tq] - lam * (acc[:, tq:] / l[:, tq:])
            o_ref[self.rows, :] = (_rms(o.T, gsub_ref[...]) * (1.0 - lam_init)).astype(BF16)

    n_sub = q_ref.shape[0] // tq
    per_sub = len(scratch) // n_sub
    blocks = [QueryBlock(pl.ds(sub * tq, tq), *scratch[sub * per_sub:(sub + 1) * per_sub])
              for sub in range(n_sub)]
    for qb in blocks:
        qb.start()
    if n_ring_blocks:
        ahead = blocks[0].ahead

        def body(j, carry):
            base = pl.multiple_of(j * (K_UNROLL * K_BLOCK), K_UNROLL * K_BLOCK)
            for u in range(K_UNROLL):
                for qb in blocks:
                    qb.ring_step(base, u, K_UNROLL + ahead)
            return carry
        n_groups = (n_ring_blocks - ahead) // K_UNROLL
        lax.fori_loop(0, n_groups, body, 0)
        for blk in range(n_groups * K_UNROLL, n_ring_blocks):
            for qb in blocks:
                qb.ring_step(0, blk, n_ring_blocks - 1)
    for qb in blocks:
        qb.finish()


def _attn_call(qk, vt, lam_vecs, g_subln, *, n_heads, dk, n_lat, n_ctx, ctx_queries, lam_init):
    b, _, t, _ = qk.shape
    dv_ext = vt.shape[2]
    dv = dv_ext - V_PAD_ROWS
    if ctx_queries:
        tq, n_sub, n_steps, q_off, n_keys, k_off = n_ctx, 1, 1, n_lat // n_ctx, n_ctx, n_lat // n_ctx
        first_block, n_ring_blocks = (0, n_ctx), 0
    else:
        tq, n_sub, n_steps, q_off, n_keys, k_off = (
            Q_BLOCK, Q_PER_STEP, n_lat // (Q_BLOCK * Q_PER_STEP), 0, t, 0)
        first_block, n_ring_blocks = (n_lat, n_ctx), n_lat // K_BLOCK
    rows = tq * n_sub
    kern = functools.partial(_attn_kernel, dk=dk, tq=tq, first_block=first_block,
                             n_ring_blocks=n_ring_blocks, lam_init=lam_init)

    def score_bufs(n_keys):
        return [pltpu.VMEM((n_keys, 2 * tq), F32), pltpu.VMEM((1, 2 * tq), F32)]

    scratch_set = [
        pltpu.VMEM((2 * dk, 2 * tq), BF16),
        pltpu.VMEM((1, 2 * tq), F32),
        pltpu.VMEM((dv_ext, 2 * tq), F32),
        *score_bufs(first_block[1]),
        *((K_RING * score_bufs(K_BLOCK)) if n_ring_blocks else []),
    ]
    return pl.pallas_call(
        kern,
        grid=(b, n_heads, n_steps),
        in_specs=[
            pl.BlockSpec((None, None, rows, LANES), lambda bi, h, i: (bi, h, q_off + i, 0)),
            pl.BlockSpec((None, None, n_keys, LANES), lambda bi, h, i: (bi, n_heads + h, k_off, 0)),
            pl.BlockSpec((None, None, dv_ext, n_keys), lambda bi, h, i: (bi, h, 0, k_off)),
            pl.BlockSpec(lam_vecs.shape, lambda bi, h, i: (0, 0)),
            pl.BlockSpec((1, dv), lambda bi, h, i: (0, 0)),
        ],
        out_specs=pl.BlockSpec((None, rows, dv), lambda bi, h, i: (bi, i, h)),
        out_shape=jax.ShapeDtypeStruct((b, n_steps * rows, n_heads * dv), BF16),
        scratch_shapes=n_sub * scratch_set,
        compiler_params=_params(3),
        name="diff_attn_ctx" if ctx_queries else "diff_attn",
    )(qk, qk, vt, lam_vecs, g_subln)


def _attn_out_ffn_kernel(o_ref, *refs, n_lat_blocks):
    x, (mod_ref, gpost_ref, gfpre_ref, gfpost_ref, wo_ref, win_ref, wout_ref, out_ref) = (
        _token_rows(refs, 8, n_lat_blocks))
    for g in range(out_ref.shape[0] // ROW_BLOCK):
        rows = slice(g * ROW_BLOCK, (g + 1) * ROW_BLOCK)
        y = jnp.dot(o_ref[rows, :], wo_ref[...], preferred_element_type=F32)
        out_ref[rows, :] = _residual_ffn(x[rows, :], y, mod_ref, gpost_ref, gfpre_ref,
                                         gfpost_ref, win_ref, wout_ref)


def _attn_out_ffn_call(o, streams, mods, g_post, g_fpre, g_fpost, wo, win, wout, *, n_lat):
    b, rows, d = o.shape[0], o.shape[1], streams[0].shape[2]
    tm = _row_block(rows, n_lat)
    n_lat_blocks = n_lat // tm
    return pl.pallas_call(
        functools.partial(_attn_out_ffn_kernel, n_lat_blocks=n_lat_blocks),
        grid=(b, rows // tm),
        in_specs=[
            pl.BlockSpec((None, tm, o.shape[2]), lambda bi, r: (bi, r, 0)),
            *_stream_specs(streams, tm, n_lat_blocks),
            _mod_spec(d, n_lat_blocks),
            _const_spec((1, d)), _const_spec((1, d)), _const_spec((1, d)),
            _const_spec(wo.shape), _const_spec(win.shape), _const_spec(wout.shape),
        ],
        out_specs=pl.BlockSpec((None, tm, d), lambda bi, r: (bi, r, 0)),
        out_shape=jax.ShapeDtypeStruct((b, rows, d), F32),
        compiler_params=_params(2),
        name="attn_out_ffn",
    )(o, *streams, mods, g_post, g_fpre, g_fpost, wo, win, wout)


def _cmlp_ffn_kernel(x_ref, mod_ref, gpre_ref, wuv_ref, buv_ref, gsgu_ref, bsgu_ref,
                     wsp_ref, bsp_ref, wco_ref, gpost_ref, gfpre_ref, gfpost_ref,
                     win_ref, wout_ref, out_ref):
    groups, chunk, _ = wsp_ref.shape
    for rg in range(x_ref.shape[0] // ROW_BLOCK):
        rows = pl.ds(rg * ROW_BLOCK, ROW_BLOCK)
        x = x_ref[rows, :]
        h = _modulated(x, gpre_ref[...], mod_ref[0:1, :], mod_ref[1:2, :]).astype(BF16)
        z = jnp.dot(h, wuv_ref[...], preferred_element_type=F32) + buv_ref[...]
        z = 0.5 * z * (1.0 + lax.erf(z * math.sqrt(0.5)))
        w = z.shape[1] // 2
        u, v = z[:, :w], z[:, w:]
        mu = jnp.mean(v, axis=-1, keepdims=True)
        vc = v - mu
        var = jnp.mean(vc * vc, axis=-1, keepdims=True)
        vb = (vc * lax.rsqrt(var + EPS) * gsgu_ref[...] + bsgu_ref[...]).astype(BF16)

        gd = w // groups
        n_chunks = ROW_BLOCK // chunk
        mixed = []
        for g in range(groups):
            rhs = jnp.concatenate(
                [vb[n * chunk:(n + 1) * chunk, g * gd:(g + 1) * gd] for n in range(n_chunks)],
                axis=1)
            mixed.append(jnp.dot(wsp_ref[g], rhs, preferred_element_type=F32) + bsp_ref[g])
        sv = jnp.concatenate(
            [jnp.concatenate([mixed[g][:, n * gd:(n + 1) * gd] for g in range(groups)], axis=1)
             for n in range(n_chunks)], axis=0)
        y = jnp.dot((u * sv).astype(BF16), wco_ref[...], preferred_element_type=F32)
        out_ref[rows, :] = _residual_ffn(x, y, mod_ref, gpost_ref, gfpre_ref, gfpost_ref,
                                         win_ref, wout_ref)


def _cmlp_ffn_call(xt, mods, g_pre, wuv, buv, g_sgu, b_sgu, wsp, bsp, wco,
                   g_post, g_fpre, g_fpost, win, wout, *, n_lat):
    b, rows, d = xt.shape
    tm = _row_block(rows, n_lat)
    n_lat_blocks = n_lat // tm
    vec = lambda a: _const_spec((1, a.shape[-1]))
    return pl.pallas_call(
        _cmlp_ffn_kernel,
        grid=(b, rows // tm),
        in_specs=[
            pl.BlockSpec((None, tm, d), lambda bi, r: (bi, r, 0)),
            _mod_spec(d, n_lat_blocks),
            vec(g_pre), _const_spec(wuv.shape), vec(buv), vec(g_sgu), vec(b_sgu),
            _const_spec(wsp.shape), _const_spec(bsp.shape), _const_spec(wco.shape),
            vec(g_post), vec(g_fpre), vec(g_fpost),
            _const_spec(win.shape), _const_spec(wout.shape),
        ],
        out_specs=pl.BlockSpec((None, tm, d), lambda bi, r: (bi, r, 0)),
        out_shape=jax.ShapeDtypeStruct((b, rows, d), F32),
        compiler_params=_params(2),
        name="cmlp_ffn",
    )(xt, mods, g_pre, wuv, buv, g_sgu, b_sgu, wsp, bsp, wco, g_post, g_fpre, g_fpost, win, wout)


def _rope_tables(n_lat, n_ctx, dk):
    axis_dim = dk // 2
    rot = axis_dim // 2
    tok = np.arange(n_lat)
    row = (tok // ROPE_GRID_W).astype(np.float32)
    col = (tok % ROPE_GRID_W).astype(np.float32)
    inv_freq = (np.float32(ROPE_BASE)
                ** (-np.arange(0, axis_dim, 2, dtype=np.float32) / np.float32(axis_dim)))
    in_map = np.arange(LANES) % dk
    freq = inv_freq[in_map % rot].astype(np.float32)
    pos = np.where((in_map < axis_dim)[None, :], row[:, None], col[:, None])
    ang = (pos * freq[None, :]).astype(np.float32)
    first = ((in_map % axis_dim) < rot)[None, :]
    cos, sin = np.cos(ang), np.sin(ang)
    sa = np.where(first, -sin, 0.0)
    sb = np.where(first, 0.0, sin)
    pad = lambda a, v: jnp.asarray(np.concatenate(
        [a, np.full((n_ctx, LANES), v)], axis=0).astype(np.float32))
    return pad(cos, 1.0), pad(sa, 0.0), pad(sb, 0.0), rot


def kernel(x, c, ctx, c_ctx, w_ada, b_ada, g_mix_pre, g_mix_post, g_ffn_pre, g_ffn_post, w_qkv, w_attn_out, lam_q1, lam_k1, lam_q2, lam_k2, g_subln, w_uv, b_uv, g_sgu, b_sgu, w_spatial, b_spatial, w_cmlp_out, w_ffn_in, w_ffn_out):
    b, s, d = x.shape
    n_ctx = ctx.shape[1]
    depth = w_ada.shape[0]
    dk = lam_q1.shape[-1]
    dv = g_subln.shape[-1]
    n_heads = w_attn_out.shape[1] // dv
    qk_w = 2 * n_heads * dk * 2
    assert 2 * dk == LANES and dv == LANES
    assert K_UNROLL % K_RING == 0 and s % K_BLOCK == 0 and s % (Q_BLOCK * Q_PER_STEP) == 0 and s % n_ctx == 0
    assert s % ROW_BLOCK == 0 and n_ctx % ROW_BLOCK == 0 and b + 1 <= SUBLANES

    cond = jnp.zeros((SUBLANES, d), F32).at[:b].set(c).at[b].set(c_ctx)
    mod_all = _ada_call(cond, w_ada, b_ada)
    mod_lat = mod_all[:, :b].reshape(depth, b, 1, N_MOD, d)
    mod_ctx = jnp.broadcast_to(mod_all[:, b].reshape(depth, 1, 1, N_MOD, d), mod_lat.shape)
    mods = jnp.concatenate([mod_lat, mod_ctx], axis=2)

    cos, sa, sb, rot = _rope_tables(s, n_ctx, dk)
    last_attn = max(i for i in range(depth) if i % N_MIXERS == 0)
    row_vec = lambda a: a.reshape(1, -1)

    streams = (x, ctx)
    for i in range(depth):
        j = i // N_MIXERS
        ctx_live = i < last_attn
        win = w_ffn_in[i].astype(BF16)
        wout = w_ffn_out[i].astype(BF16)
        tail = (row_vec(g_mix_post[i]), row_vec(g_ffn_pre[i]), row_vec(g_ffn_post[i]))
        if i % N_MIXERS == 0:
            lam_init = 0.8 - 0.6 * math.exp(-0.3 * i)
            wqk = w_qkv[j][:, :qk_w].astype(BF16)
            wvt = w_qkv[j][:, qk_w:].T.astype(BF16)
            qk, vt = _qkv_call(streams, mods[i], row_vec(g_mix_pre[i]), wqk, wvt, cos, sa, sb,
                               n_lat=s, rot=rot, dk=dk, n_heads=n_heads)
            lam_vecs = jnp.stack([lam_q1[j], lam_k1[j], lam_q2[j], lam_k2[j]]).astype(F32)
            attend = functools.partial(_attn_call, qk, vt, lam_vecs, row_vec(g_subln[j]),
                                       n_heads=n_heads, dk=dk, n_lat=s, n_ctx=n_ctx,
                                       lam_init=lam_init)
            o = attend(ctx_queries=False)
            if ctx_live:
                o = jnp.concatenate([o, attend(ctx_queries=True)], axis=1)
            streams = (_attn_out_ffn_call(o, streams, mods[i], *tail, w_attn_out[j].astype(BF16),
                                          win, wout, n_lat=s),)
        else:
            xt, = streams
            if not ctx_live and xt.shape[1] != s:
                xt = xt[:, :s]
            streams = (_cmlp_ffn_call(
                xt, mods[i], row_vec(g_mix_pre[i]), w_uv[j].astype(BF16), row_vec(b_uv[j]),
                row_vec(g_sgu[j]), row_vec(b_sgu[j]), w_spatial[j].astype(BF16),
                b_spatial[j][:, :, None], w_cmlp_out[j].astype(BF16), *tail, win, wout,
                n_lat=s),)
    xt, = streams
    return xt[:, :s] if xt.shape[1] != s else xt
```

```python
import functools
import math

import jax
import jax.numpy as jnp
import numpy as np
from jax import lax
from jax.experimental import pallas as pl
from jax.experimental.pallas import tpu as pltpu

F32 = jnp.float32
BF16 = jnp.bfloat16

EPS = 1e-6
ROPE_GRID_W = 64
ROPE_BASE = 10000.0
N_MIXERS = 2
N_MOD = 6

LANES = 128
SUBLANES = 8
VMEM_LIMIT_BYTES = 56 << 20

ROW_BLOCK = 256
ROW_GROUPS = 2
Q_BLOCK = 256
Q_PER_STEP = 4
K_BLOCK = 1024
V_PAD_ROWS = 16
K_RING = 2
K_UNROLL = 6


def _const_spec(shape):
    return pl.BlockSpec(shape, lambda *_: (0,) * len(shape), pipeline_mode=pl.Buffered(1))


def _params(n_grid, flags=None):
    return pltpu.CompilerParams(
        dimension_semantics=("arbitrary",) * n_grid, vmem_limit_bytes=VMEM_LIMIT_BYTES,
        flags=flags)


def _rms(x, g):
    return x * lax.rsqrt(jnp.mean(x * x, axis=-1, keepdims=True) + EPS) * g


def _modulated(x, g, shift, scale):
    return _rms(x, g) * (1.0 + scale) + shift


def _ada_kernel(c_ref, w_ref, b_ref, o_ref):
    s = jax.nn.silu(c_ref[...])
    o_ref[...] = jnp.dot(s, w_ref[...], preferred_element_type=F32,
                         precision=lax.Precision.HIGHEST) + b_ref[...]


def _ada_call(cond, w_ada, b_ada):
    depth, d, n = w_ada.shape
    tn = 2048
    return pl.pallas_call(
        _ada_kernel,
        grid=(depth, n // tn),
        in_specs=[
            pl.BlockSpec(cond.shape, lambda l, k: (0, 0)),
            pl.BlockSpec((None, d, tn), lambda l, k: (l, 0, k)),
            pl.BlockSpec((None, 1, tn), lambda l, k: (l, 0, k)),
        ],
        out_specs=pl.BlockSpec((None, cond.shape[0], tn), lambda l, k: (l, 0, k)),
        out_shape=jax.ShapeDtypeStruct((depth, cond.shape[0], n), F32),
        compiler_params=_params(2),
        name="ada_mod",
    )(cond, w_ada, b_ada.reshape(depth, 1, n))


def _residual_ffn(x, y, mod_ref, gpost_ref, gfpre_ref, gfpost_ref, win_ref, wout_ref):
    x = x + mod_ref[2:3, :] * _rms(y, gpost_ref[...])
    h = _modulated(x, gfpre_ref[...], mod_ref[3:4, :], mod_ref[4:5, :]).astype(BF16)
    gu = jnp.dot(h, win_ref[...], preferred_element_type=F32)
    f = wout_ref.shape[0]
    a = (jax.nn.silu(gu[:, :f]) * gu[:, f:]).astype(BF16)
    y2 = jnp.dot(a, wout_ref[...], preferred_element_type=F32)
    return x + mod_ref[5:6, :] * _rms(y2, gfpost_ref[...])


def _row_block(rows, n_lat):
    wide = ROW_BLOCK * ROW_GROUPS
    return wide if rows == n_lat and rows % wide == 0 else ROW_BLOCK


def _mod_spec(d, n_lat_blocks):
    return pl.BlockSpec((None, None, N_MOD, d),
                        lambda b, r: (b, jnp.minimum(r // n_lat_blocks, 1), 0, 0))


def _stream_specs(streams, tm, n_lat_blocks):
    d = streams[0].shape[2]
    if len(streams) == 1:
        return [pl.BlockSpec((None, tm, d), lambda bi, r: (bi, r, 0))]
    return [pl.BlockSpec((None, tm, d), lambda bi, r: (bi, jnp.minimum(r, n_lat_blocks - 1), 0)),
            pl.BlockSpec((None, tm, d), lambda bi, r: (bi, jnp.maximum(r - n_lat_blocks, 0), 0))]


def _token_rows(refs, n_rest, n_lat_blocks):
    stream_refs, rest = refs[:len(refs) - n_rest], refs[len(refs) - n_rest:]
    if len(stream_refs) == 1:
        return stream_refs[0][...], rest
    lat_ref, ctx_ref = stream_refs
    return jnp.where(pl.program_id(1) >= n_lat_blocks, ctx_ref[...], lat_ref[...]), rest


def _qkv_kernel(*refs, rot, q_scale, n_lat_blocks):
    x, (mod_ref, g_ref, wqk_ref, wvt_ref, cos_ref, sa_ref, sb_ref, qk_ref, vt_ref) = (
        _token_rows(refs, 9, n_lat_blocks))
    h = _modulated(x, g_ref[...], mod_ref[0:1, :], mod_ref[1:2, :]).astype(BF16)
    qk = jnp.dot(h, wqk_ref[...], preferred_element_type=F32)
    cos, sa, sb = cos_ref[...], sa_ref[...], sb_ref[...]
    n_blk = qk.shape[1] // LANES
    for c in range(n_blk):
        blk = qk[:, c * LANES:(c + 1) * LANES]
        y = blk * cos + pltpu.roll(blk, LANES - rot, 1) * sa + pltpu.roll(blk, rot, 1) * sb
        if c < n_blk // 2:
            y = y * q_scale
        qk_ref[c] = y.astype(BF16)
    vt = lax.dot_general(wvt_ref[...], h, (((1,), (1,)), ((), ())), preferred_element_type=F32)
    n_heads, dv_ext, tm = vt_ref.shape
    dv = vt.shape[0] // n_heads
    pad_row = lax.broadcasted_iota(jnp.int32, (dv_ext - dv, tm), 0)
    pad = jnp.where(pad_row == 0, 1.0, 0.0).astype(BF16)
    for hd in range(n_heads):
        vt_ref[hd, :dv, :] = vt[hd * dv:(hd + 1) * dv, :].astype(BF16)
        vt_ref[hd, dv:, :] = pad


def _qkv_call(streams, mods, g_pre, wqk, wvt, cos, sa, sb, *, n_lat, rot, dk, n_heads):
    b, d = streams[0].shape[0], streams[0].shape[2]
    t = sum(a.shape[1] for a in streams)
    tm = ROW_BLOCK
    n_lat_blocks = n_lat // tm
    n_qk = wqk.shape[1] // LANES
    dv_ext = wvt.shape[0] // n_heads + V_PAD_ROWS
    return pl.pallas_call(
        functools.partial(_qkv_kernel, rot=rot, q_scale=dk ** -0.5 * math.log2(math.e),
                          n_lat_blocks=n_lat_blocks),
        grid=(b, t // tm),
        in_specs=[
            *_stream_specs(streams, tm, n_lat_blocks),
            _mod_spec(d, n_lat_blocks),
            _const_spec((1, d)),
            _const_spec(wqk.shape),
            _const_spec(wvt.shape),
            pl.BlockSpec((tm, LANES), lambda bi, r: (r, 0)),
            pl.BlockSpec((tm, LANES), lambda bi, r: (r, 0)),
            pl.BlockSpec((tm, LANES), lambda bi, r: (r, 0)),
        ],
        out_specs=[
            pl.BlockSpec((None, n_qk, tm, LANES), lambda bi, r: (bi, 0, r, 0)),
            pl.BlockSpec((None, n_heads, dv_ext, tm), lambda bi, r: (bi, 0, 0, r)),
        ],
        out_shape=[
            jax.ShapeDtypeStruct((b, n_qk, t, LANES), BF16),
            jax.ShapeDtypeStruct((b, n_heads, dv_ext, t), BF16),
        ],
        compiler_params=_params(2),
        name="qkv_rope",
    )(*streams, mods, g_pre, wqk, wvt, cos, sa, sb)


def _attn_kernel(q_ref, k_ref, vt_ref, lam_ref, gsub_ref, o_ref, *scratch,
                 dk, tq, first_block, n_ring_blocks, lam_init):
    lam = (jnp.exp(jnp.sum(lam_ref[0:1, :] * lam_ref[1:2, :], keepdims=True))
           - jnp.exp(jnp.sum(lam_ref[2:3, :] * lam_ref[3:4, :], keepdims=True)) + lam_init)
    dv = o_ref.shape[1]

    class QueryBlock:
        def __init__(self, rows, qst_ref, m_ref, acc_ref, *bufs):
            self.rows, self.qst_ref, self.m_ref, self.acc_ref = rows, qst_ref, m_ref, acc_ref
            self.first_buf = bufs[0:2]
            self.ring = tuple(bufs[i:i + 2] for i in range(2, len(bufs), 2))
            self.ahead = min(len(self.ring) - 1, n_ring_blocks)

        def start(self):
            qt = q_ref[self.rows, :].astype(F32).T
            row = lax.broadcasted_iota(jnp.int32, qt.shape, 0)
            self.qst_ref[:, :tq] = jnp.where(row < dk, qt, 0.0).astype(BF16)
            self.qst_ref[:, tq:] = jnp.where(row >= dk, qt, 0.0).astype(BF16)
            self.m_ref[...] = jnp.full(self.m_ref.shape, -jnp.inf, F32)
            self.acc_ref[...] = jnp.zeros(self.acc_ref.shape, F32)
            self.scores(*first_block, self.first_buf)
            for blk in range(self.ahead):
                self.scores(blk * K_BLOCK, K_BLOCK, self.ring[blk])
            self.absorb(*first_block, self.first_buf)

        def scores(self, start, size, buf):
            s_ref, mx_ref = buf
            s = jnp.dot(k_ref[pl.ds(start, size), :], self.qst_ref[...],
                        preferred_element_type=F32)
            s_ref[...] = s
            mx_ref[...] = jnp.max(s, axis=0, keepdims=True)

        def absorb(self, start, size, buf):
            s_ref, mx_ref = buf
            m_old = self.m_ref[...]
            m_new = jnp.maximum(m_old, mx_ref[...])
            alpha = jnp.exp2(m_old - m_new)
            p = jnp.exp2(s_ref[...] - m_new)
            pv = jnp.dot(vt_ref[:, pl.ds(start, size)], p.astype(BF16),
                         preferred_element_type=F32)
            self.acc_ref[...] = alpha * self.acc_ref[...] + pv
            self.m_ref[...] = m_new

        def ring_step(self, base, blk, last_scored):
            n = len(self.ring)
            if blk + self.ahead <= last_scored:
                self.scores(base + (blk + self.ahead) * K_BLOCK, K_BLOCK,
                            self.ring[(blk + self.ahead) % n])
            self.absorb(base + blk * K_BLOCK, K_BLOCK, self.ring[blk % n])

        def finish(self):
            acc = self.acc_ref[:dv, :]
            l = self.acc_ref[dv:dv + 1, :]
            o = acc[:, :tq] / l[:, :tq] - lam * (acc[:, tq:] / l[:, tq:])
            o_ref[self.rows, :] = (_rms(o.T, gsub_ref[...]) * (1.0 - lam_init)).astype(BF16)

    n_sub = q_ref.shape[0] // tq
    per_sub = len(scratch) // n_sub
    blocks = [QueryBlock(pl.ds(sub * tq, tq), *scratch[sub * per_sub:(sub + 1) * per_sub])
              for sub in range(n_sub)]
    for qb in blocks:
        qb.start()
    if n_ring_blocks:
        ahead = blocks[0].ahead

        def body(j, carry):
            base = pl.multiple_of(j * (K_UNROLL * K_BLOCK), K_UNROLL * K_BLOCK)
            for u in range(K_UNROLL):
                for qb in blocks:
                    qb.ring_step(base, u, K_UNROLL + ahead)
            return carry
        n_groups = (n_ring_blocks - ahead) // K_UNROLL
        lax.fori_loop(0, n_groups, body, 0)
        for blk in range(n_groups * K_UNROLL, n_ring_blocks):
            for qb in blocks:
                qb.ring_step(0, blk, n_ring_blocks - 1)
    for qb in blocks:
        qb.finish()


def _attn_call(qk, vt, lam_vecs, g_subln, *, n_heads, dk, n_lat, n_ctx, ctx_queries, lam_init):
    b, _, t, _ = qk.shape
    dv_ext = vt.shape[2]
    dv = dv_ext - V_PAD_ROWS
    if ctx_queries:
        tq, n_sub, n_steps, q_off, n_keys, k_off = n_ctx, 1, 1, n_lat // n_ctx, n_ctx, n_lat // n_ctx
        first_block, n_ring_blocks = (0, n_ctx), 0
    else:
        tq, n_sub, n_steps, q_off, n_keys, k_off = (
            Q_BLOCK, Q_PER_STEP, n_lat // (Q_BLOCK * Q_PER_STEP), 0, t, 0)
        first_block, n_ring_blocks = (n_lat, n_ctx), n_lat // K_BLOCK
    rows = tq * n_sub
    kern = functools.partial(_attn_kernel, dk=dk, tq=tq, first_block=first_block,
                             n_ring_blocks=n_ring_blocks, lam_init=lam_init)

    def score_bufs(n_keys):
        return [pltpu.VMEM((n_keys, 2 * tq), F32), pltpu.VMEM((1, 2 * tq), F32)]

    scratch_set = [
        pltpu.VMEM((2 * dk, 2 * tq), BF16),
        pltpu.VMEM((1, 2 * tq), F32),
        pltpu.VMEM((dv_ext, 2 * tq), F32),
        *score_bufs(first_block[1]),
        *((K_RING * score_bufs(K_BLOCK)) if n_ring_blocks else []),
    ]
    return pl.pallas_call(
        kern,
        grid=(b, n_heads, n_steps),
        in_specs=[
            pl.BlockSpec((None, None, rows, LANES), lambda bi, h, i: (bi, h, q_off + i, 0)),
            pl.BlockSpec((None, None, n_keys, LANES), lambda bi, h, i: (bi, n_heads + h, k_off, 0)),
            pl.BlockSpec((None, None, dv_ext, n_keys), lambda bi, h, i: (bi, h, 0, k_off)),
            pl.BlockSpec(lam_vecs.shape, lambda bi, h, i: (0, 0)),
            pl.BlockSpec((1, dv), lambda bi, h, i: (0, 0)),
        ],
        out_specs=pl.BlockSpec((None, rows, dv), lambda bi, h, i: (bi, i, h)),
        out_shape=jax.ShapeDtypeStruct((b, n_steps * rows, n_heads * dv), BF16),
        scratch_shapes=n_sub * scratch_set,
        compiler_params=_params(3),
        name="diff_attn_ctx" if ctx_queries else "diff_attn",
    )(qk, qk, vt, lam_vecs, g_subln)


def _attn_out_ffn_kernel(o_ref, *refs, n_lat_blocks):
    x, (mod_ref, gpost_ref, gfpre_ref, gfpost_ref, wo_ref, win_ref, wout_ref, out_ref) = (
        _token_rows(refs, 8, n_lat_blocks))
    for g in range(out_ref.shape[0] // ROW_BLOCK):
        rows = slice(g * ROW_BLOCK, (g + 1) * ROW_BLOCK)
        y = jnp.dot(o_ref[rows, :], wo_ref[...], preferred_element_type=F32)
        out_ref[rows, :] = _residual_ffn(x[rows, :], y, mod_ref, gpost_ref, gfpre_ref,
                                         gfpost_ref, win_ref, wout_ref)


def _attn_out_ffn_call(o, streams, mods, g_post, g_fpre, g_fpost, wo, win, wout, *, n_lat):
    b, rows, d = o.shape[0], o.shape[1], streams[0].shape[2]
    tm = _row_block(rows, n_lat)
    n_lat_blocks = n_lat // tm
    return pl.pallas_call(
        functools.partial(_attn_out_ffn_kernel, n_lat_blocks=n_lat_blocks),
        grid=(b, rows // tm),
        in_specs=[
            pl.BlockSpec((None, tm, o.shape[2]), lambda bi, r: (bi, r, 0)),
            *_stream_specs(streams, tm, n_lat_blocks),
            _mod_spec(d, n_lat_blocks),
            _const_spec((1, d)), _const_spec((1, d)), _const_spec((1, d)),
            _const_spec(wo.shape), _const_spec(win.shape), _const_spec(wout.shape),
        ],
        out_specs=pl.BlockSpec((None, tm, d), lambda bi, r: (bi, r, 0)),
        out_shape=jax.ShapeDtypeStruct((b, rows, d), F32),
        compiler_params=_params(2),
        name="attn_out_ffn",
    )(o, *streams, mods, g_post, g_fpre, g_fpost, wo, win, wout)


def _cmlp_ffn_kernel(x_ref, mod_ref, gpre_ref, wuv_ref, buv_ref, gsgu_ref, bsgu_ref,
                     wsp_ref, bsp_ref, wco_ref, gpost_ref, gfpre_ref, gfpost_ref,
                     win_ref, wout_ref, out_ref):
    groups, chunk, _ = wsp_ref.shape
    for rg in range(x_ref.shape[0] // ROW_BLOCK):
        rows = pl.ds(rg * ROW_BLOCK, ROW_BLOCK)
        x = x_ref[rows, :]
        h = _modulated(x, gpre_ref[...], mod_ref[0:1, :], mod_ref[1:2, :]).astype(BF16)
        z = jnp.dot(h, wuv_ref[...], preferred_element_type=F32) + buv_ref[...]
        z = 0.5 * z * (1.0 + lax.erf(z * math.sqrt(0.5)))
        w = z.shape[1] // 2
        u, v = z[:, :w], z[:, w:]
        mu = jnp.mean(v, axis=-1, keepdims=True)
        vc = v - mu
        var = jnp.mean(vc * vc, axis=-1, keepdims=True)
        vb = (vc * lax.rsqrt(var + EPS) * gsgu_ref[...] + bsgu_ref[...]).astype(BF16)

        gd = w // groups
        n_chunks = ROW_BLOCK // chunk
        mixed = []
        for g in range(groups):
            rhs = jnp.concatenate(
                [vb[n * chunk:(n + 1) * chunk, g * gd:(g + 1) * gd] for n in range(n_chunks)],
                axis=1)
            mixed.append(jnp.dot(wsp_ref[g], rhs, preferred_element_type=F32) + bsp_ref[g])
        sv = jnp.concatenate(
            [jnp.concatenate([mixed[g][:, n * gd:(n + 1) * gd] for g in range(groups)], axis=1)
             for n in range(n_chunks)], axis=0)
        y = jnp.dot((u * sv).astype(BF16), wco_ref[...], preferred_element_type=F32)
        out_ref[rows, :] = _residual_ffn(x, y, mod_ref, gpost_ref, gfpre_ref, gfpost_ref,
                                         win_ref, wout_ref)


def _cmlp_ffn_call(xt, mods, g_pre, wuv, buv, g_sgu, b_sgu, wsp, bsp, wco,
                   g_post, g_fpre, g_fpost, win, wout, *, n_lat):
    b, rows, d = xt.shape
    tm = _row_block(rows, n_lat)
    n_lat_blocks = n_lat // tm
    vec = lambda a: _const_spec((1, a.shape[-1]))
    return pl.pallas_call(
        _cmlp_ffn_kernel,
        grid=(b, rows // tm),
        in_specs=[
            pl.BlockSpec((None, tm, d), lambda bi, r: (bi, r, 0)),
            _mod_spec(d, n_lat_blocks),
            vec(g_pre), _const_spec(wuv.shape), vec(buv), vec(g_sgu), vec(b_sgu),
            _const_spec(wsp.shape), _const_spec(bsp.shape), _const_spec(wco.shape),
            vec(g_post), vec(g_fpre), vec(g_fpost),
            _const_spec(win.shape), _const_spec(wout.shape),
        ],
        out_specs=pl.BlockSpec((None, tm, d), lambda bi, r: (bi, r, 0)),
        out_shape=jax.ShapeDtypeStruct((b, rows, d), F32),
        compiler_params=_params(2),
        name="cmlp_ffn",
    )(xt, mods, g_pre, wuv, buv, g_sgu, b_sgu, wsp, bsp, wco, g_post, g_fpre, g_fpost, win, wout)


def _rope_tables(n_lat, n_ctx, dk):
    axis_dim = dk // 2
    rot = axis_dim // 2
    tok = np.arange(n_lat)
    row = (tok // ROPE_GRID_W).astype(np.float32)
    col = (tok % ROPE_GRID_W).astype(np.float32)
    inv_freq = (np.float32(ROPE_BASE)
                ** (-np.arange(0, axis_dim, 2, dtype=np.float32) / np.float32(axis_dim)))
    in_map = np.arange(LANES) % dk
    freq = inv_freq[in_map % rot].astype(np.float32)
    pos = np.where((in_map < axis_dim)[None, :], row[:, None], col[:, None])
    ang = (pos * freq[None, :]).astype(np.float32)
    first = ((in_map % axis_dim) < rot)[None, :]
    cos, sin = np.cos(ang), np.sin(ang)
    sa = np.where(first, -sin, 0.0)
    sb = np.where(first, 0.0, sin)
    pad = lambda a, v: jnp.asarray(np.concatenate(
        [a, np.full((n_ctx, LANES), v)], axis=0).astype(np.float32))
    return pad(cos, 1.0), pad(sa, 0.0), pad(sb, 0.0), rot


def kernel(x, c, ctx, c_ctx, w_ada, b_ada, g_mix_pre, g_mix_post, g_ffn_pre, g_ffn_post, w_qkv, w_attn_out, lam_q1, lam_k1, lam_q2, lam_k2, g_subln, w_uv, b_uv, g_sgu, b_sgu, w_spatial, b_spatial, w_cmlp_out, w_ffn_in, w_ffn_out):
    b, s, d = x.shape
    n_ctx = ctx.shape[1]
    depth = w_ada.shape[0]
    dk = lam_q1.shape[-1]
    dv = g_subln.shape[-1]
    n_heads = w_attn_out.shape[1] // dv
    qk_w = 2 * n_heads * dk * 2
    assert 2 * dk == LANES and dv == LANES
    assert K_UNROLL % K_RING == 0 and s % K_BLOCK == 0 and s % (Q_BLOCK * Q_PER_STEP) == 0 and s % n_ctx == 0
    assert s % ROW_BLOCK == 0 and n_ctx % ROW_BLOCK == 0 and b + 1 <= SUBLANES

    cond = jnp.zeros((SUBLANES, d), F32).at[:b].set(c).at[b].set(c_ctx)
    mod_all = _ada_call(cond, w_ada, b_ada)
    mod_lat = mod_all[:, :b].reshape(depth, b, 1, N_MOD, d)
    mod_ctx = jnp.broadcast_to(mod_all[:, b].reshape(depth, 1, 1, N_MOD, d), mod_lat.shape)
    mods = jnp.concatenate([mod_lat, mod_ctx], axis=2)

    cos, sa, sb, rot = _rope_tables(s, n_ctx, dk)
    last_attn = max(i for i in range(depth) if i % N_MIXERS == 0)
    row_vec = lambda a: a.reshape(1, -1)

    streams = (x, ctx)
    for i in range(depth):
        j = i // N_MIXERS
        ctx_live = i < last_attn
        win = w_ffn_in[i].astype(BF16)
        wout = w_ffn_out[i].astype(BF16)
        tail = (row_vec(g_mix_post[i]), row_vec(g_ffn_pre[i]), row_vec(g_ffn_post[i]))
        if i % N_MIXERS == 0:
            lam_init = 0.8 - 0.6 * math.exp(-0.3 * i)
            wqk = w_qkv[j][:, :qk_w].astype(BF16)
            wvt = w_qkv[j][:, qk_w:].T.astype(BF16)
            qk, vt = _qkv_call(streams, mods[i], row_vec(g_mix_pre[i]), wqk, wvt, cos, sa, sb,
                               n_lat=s, rot=rot, dk=dk, n_heads=n_heads)
            lam_vecs = jnp.stack([lam_q1[j], lam_k1[j], lam_q2[j], lam_k2[j]]).astype(F32)
            attend = functools.partial(_attn_call, qk, vt, lam_vecs, row_vec(g_subln[j]),
                                       n_heads=n_heads, dk=dk, n_lat=s, n_ctx=n_ctx,
                                       lam_init=lam_init)
            o = attend(ctx_queries=False)
            if ctx_live:
                o = jnp.concatenate([o, attend(ctx_queries=True)], axis=1)
            streams = (_attn_out_ffn_call(o, streams, mods[i], *tail, w_attn_out[j].astype(BF16),
                                          win, wout, n_lat=s),)
        else:
            xt, = streams
            if not ctx_live and xt.shape[1] != s:
                xt = xt[:, :s]
            streams = (_cmlp_ffn_call(
                xt, mods[i], row_vec(g_mix_pre[i]), w_uv[j].astype(BF16), row_vec(b_uv[j]),
                row_vec(g_sgu[j]), row_vec(b_sgu[j]), w_spatial[j].astype(BF16),
                b_spatial[j][:, :, None], w_cmlp_out[j].astype(BF16), *tail, win, wout,
                n_lat=s),)
    xt, = streams
    return xt[:, :s] if xt.shape[1] != s else xt
```

```python
import functools
import math

import jax
import jax.numpy as jnp
import numpy as np
from jax import lax
from jax.experimental import pallas as pl
from jax.experimental.pallas import tpu as pltpu

F32 = jnp.float32
BF16 = jnp.bfloat16

EPS = 1e-6
ROPE_GRID_W = 64
ROPE_BASE = 10000.0
N_MIXERS = 2
N_MOD = 6

LANES = 128
SUBLANES = 8
VMEM_LIMIT_BYTES = 56 << 20

ROW_BLOCK = 256
ROW_GROUPS = 2
Q_BLOCK = 256
Q_PER_STEP = 4
K_BLOCK = 1024
V_PAD_ROWS = 16
K_RING = 3
K_UNROLL = 6


def _const_spec(shape):
    return pl.BlockSpec(shape, lambda *_: (0,) * len(shape), pipeline_mode=pl.Buffered(1))


def _params(n_grid, flags=None):
    return pltpu.CompilerParams(
        dimension_semantics=("arbitrary",) * n_grid, vmem_limit_bytes=VMEM_LIMIT_BYTES,
        flags=flags)


def _rms(x, g):
    return x * lax.rsqrt(jnp.mean(x * x, axis=-1, keepdims=True) + EPS) * g


def _modulated(x, g, shift, scale):
    return _rms(x, g) * (1.0 + scale) + shift


def _ada_kernel(c_ref, w_ref, b_ref, o_ref):
    s = jax.nn.silu(c_ref[...])
    o_ref[...] = jnp.dot(s, w_ref[...], preferred_element_type=F32,
                         precision=lax.Precision.HIGHEST) + b_ref[...]


def _ada_call(cond, w_ada, b_ada):
    depth, d, n = w_ada.shape
    tn = 2048
    return pl.pallas_call(
        _ada_kernel,
        grid=(depth, n // tn),
        in_specs=[
            pl.BlockSpec(cond.shape, lambda l, k: (0, 0)),
            pl.BlockSpec((None, d, tn), lambda l, k: (l, 0, k)),
            pl.BlockSpec((None, 1, tn), lambda l, k: (l, 0, k)),
        ],
        out_specs=pl.BlockSpec((None, cond.shape[0], tn), lambda l, k: (l, 0, k)),
        out_shape=jax.ShapeDtypeStruct((depth, cond.shape[0], n), F32),
        compiler_params=_params(2),
        name="ada_mod",
    )(cond, w_ada, b_ada.reshape(depth, 1, n))


def _residual_ffn(x, y, mod_ref, gpost_ref, gfpre_ref, gfpost_ref, win_ref, wout_ref):
    x = x + mod_ref[2:3, :] * _rms(y, gpost_ref[...])
    h = _modulated(x, gfpre_ref[...], mod_ref[3:4, :], mod_ref[4:5, :]).astype(BF16)
    gu = jnp.dot(h, win_ref[...], preferred_element_type=F32)
    f = wout_ref.shape[0]
    a = (jax.nn.silu(gu[:, :f]) * gu[:, f:]).astype(BF16)
    y2 = jnp.dot(a, wout_ref[...], preferred_element_type=F32)
    return x + mod_ref[5:6, :] * _rms(y2, gfpost_ref[...])


def _row_block(rows, n_lat):
    wide = ROW_BLOCK * ROW_GROUPS
    return wide if rows == n_lat and rows % wide == 0 else ROW_BLOCK


def _mod_spec(d, n_lat_blocks):
    return pl.BlockSpec((None, None, N_MOD, d),
                        lambda b, r: (b, jnp.minimum(r // n_lat_blocks, 1), 0, 0))


def _stream_specs(streams, tm, n_lat_blocks):
    d = streams[0].shape[2]
    if len(streams) == 1:
        return [pl.BlockSpec((None, tm, d), lambda bi, r: (bi, r, 0))]
    return [pl.BlockSpec((None, tm, d), lambda bi, r: (bi, jnp.minimum(r, n_lat_blocks - 1), 0)),
            pl.BlockSpec((None, tm, d), lambda bi, r: (bi, jnp.maximum(r - n_lat_blocks, 0), 0))]


def _token_rows(refs, n_rest, n_lat_blocks):
    stream_refs, rest = refs[:len(refs) - n_rest], refs[len(refs) - n_rest:]
    if len(stream_refs) == 1:
        return stream_refs[0][...], rest
    lat_ref, ctx_ref = stream_refs
    return jnp.where(pl.program_id(1) >= n_lat_blocks, ctx_ref[...], lat_ref[...]), rest


def _qkv_kernel(*refs, rot, q_scale, n_lat_blocks):
    x, (mod_ref, g_ref, wqk_ref, wvt_ref, cos_ref, sa_ref, sb_ref, qk_ref, vt_ref) = (
        _token_rows(refs, 9, n_lat_blocks))
    h = _modulated(x, g_ref[...], mod_ref[0:1, :], mod_ref[1:2, :]).astype(BF16)
    qk = jnp.dot(h, wqk_ref[...], preferred_element_type=F32)
    cos, sa, sb = cos_ref[...], sa_ref[...], sb_ref[...]
    n_blk = qk.shape[1] // LANES
    for c in range(n_blk):
        blk = qk[:, c * LANES:(c + 1) * LANES]
        y = blk * cos + pltpu.roll(blk, LANES - rot, 1) * sa + pltpu.roll(blk, rot, 1) * sb
        if c < n_blk // 2:
            y = y * q_scale
        qk_ref[c] = y.astype(BF16)
    vt = lax.dot_general(wvt_ref[...], h, (((1,), (1,)), ((), ())), preferred_element_type=F32)
    n_heads, dv_ext, tm = vt_ref.shape
    dv = vt.shape[0] // n_heads
    pad_row = lax.broadcasted_iota(jnp.int32, (dv_ext - dv, tm), 0)
    pad = jnp.where(pad_row == 0, 1.0, 0.0).astype(BF16)
    for hd in range(n_heads):
        vt_ref[hd, :dv, :] = vt[hd * dv:(hd + 1) * dv, :].astype(BF16)
        vt_ref[hd, dv:, :] = pad


def _qkv_call(streams, mods, g_pre, wqk, wvt, cos, sa, sb, *, n_lat, rot, dk, n_heads):
    b, d = streams[0].shape[0], streams[0].shape[2]
    t = sum(a.shape[1] for a in streams)
    tm = ROW_BLOCK
    n_lat_blocks = n_lat // tm
    n_qk = wqk.shape[1] // LANES
    dv_ext = wvt.shape[0] // n_heads + V_PAD_ROWS
    return pl.pallas_call(
        functools.partial(_qkv_kernel, rot=rot, q_scale=dk ** -0.5 * math.log2(math.e),
                          n_lat_blocks=n_lat_blocks),
        grid=(b, t // tm),
        in_specs=[
            *_stream_specs(streams, tm, n_lat_blocks),
            _mod_spec(d, n_lat_blocks),
            _const_spec((1, d)),
            _const_spec(wqk.shape),
            _const_spec(wvt.shape),
            pl.BlockSpec((tm, LANES), lambda bi, r: (r, 0)),
            pl.BlockSpec((tm, LANES), lambda bi, r: (r, 0)),
            pl.BlockSpec((tm, LANES), lambda bi, r: (r, 0)),
        ],
        out_specs=[
            pl.BlockSpec((None, n_qk, tm, LANES), lambda bi, r: (bi, 0, r, 0)),
            pl.BlockSpec((None, n_heads, dv_ext, tm), lambda bi, r: (bi, 0, 0, r)),
        ],
        out_shape=[
            jax.ShapeDtypeStruct((b, n_qk, t, LANES), BF16),
            jax.ShapeDtypeStruct((b, n_heads, dv_ext, t), BF16),
        ],
        compiler_params=_params(2),
        name="qkv_rope",
    )(*streams, mods, g_pre, wqk, wvt, cos, sa, sb)


def _attn_kernel(q_ref, k_ref, vt_ref, lam_ref, gsub_ref, o_ref, *scratch,
                 dk, tq, first_block, n_ring_blocks, lam_init):
    lam = (jnp.exp(jnp.sum(lam_ref[0:1, :] * lam_ref[1:2, :], keepdims=True))
           - jnp.exp(jnp.sum(lam_ref[2:3, :] * lam_ref[3:4, :], keepdims=True)) + lam_init)
    dv = o_ref.shape[1]

    class QueryBlock:
        def __init__(self, rows, qst_ref, m_ref, acc_ref, *bufs):
            self.rows, self.qst_ref, self.m_ref, self.acc_ref = rows, qst_ref, m_ref, acc_ref
            self.first_buf = bufs[0:2]
            self.ring = tuple(bufs[i:i + 2] for i in range(2, len(bufs), 2))
            self.ahead = min(len(self.ring) - 1, n_ring_blocks)

        def start(self):
            qt = q_ref[self.rows, :].astype(F32).T
            row = lax.broadcasted_iota(jnp.int32, qt.shape, 0)
            self.qst_ref[:, :tq] = jnp.where(row < dk, qt, 0.0).astype(BF16)
            self.qst_ref[:, tq:] = jnp.where(row >= dk, qt, 0.0).astype(BF16)
            self.m_ref[...] = jnp.full(self.m_ref.shape, -jnp.inf, F32)
            self.acc_ref[...] = jnp.zeros(self.acc_ref.shape, F32)
            self.scores(*first_block, self.first_buf)
            for blk in range(self.ahead):
                self.scores(blk * K_BLOCK, K_BLOCK, self.ring[blk])
            self.absorb(*first_block, self.first_buf)

        def scores(self, start, size, buf):
            s_ref, mx_ref = buf
            s = jnp.dot(k_ref[pl.ds(start, size), :], self.qst_ref[...],
                        preferred_element_type=F32)
            s_ref[...] = s
            mx_ref[...] = jnp.max(s, axis=0, keepdims=True)

        def absorb(self, start, size, buf):
            s_ref, mx_ref = buf
            m_old = self.m_ref[...]
            m_new = jnp.maximum(m_old, mx_ref[...])
            alpha = jnp.exp2(m_old - m_new)
            p = jnp.exp2(s_ref[...] - m_new)
            pv = jnp.dot(vt_ref[:, pl.ds(start, size)], p.astype(BF16),
                         preferred_element_type=F32)
            self.acc_ref[...] = alpha * self.acc_ref[...] + pv
            self.m_ref[...] = m_new

        def ring_step(self, base, blk, last_scored):
            n = len(self.ring)
            if blk + self.ahead <= last_scored:
                self.scores(base + (blk + self.ahead) * K_BLOCK, K_BLOCK,
                            self.ring[(blk + self.ahead) % n])
            self.absorb(base + blk * K_BLOCK, K_BLOCK, self.ring[blk % n])

        def finish(self):
            acc = self.acc_ref[:dv, :]
            l = self.acc_ref[dv:dv + 1, :]
            o = acc[:, :tq] / l[:, :tq] - lam * (acc[:, tq:] / l[:, tq:])
            o_ref[self.rows, :] = (_rms(o.T, gsub_ref[...]) * (1.0 - lam_init)).astype(BF16)

    n_sub = q_ref.shape[0] // tq
    per_sub = len(scratch) // n_sub
    blocks = [QueryBlock(pl.ds(sub * tq, tq), *scratch[sub * per_sub:(sub + 1) * per_sub])
              for sub in range(n_sub)]
    for qb in blocks:
        qb.start()
    if n_ring_blocks:
        ahead = blocks[0].ahead

        def body(j, carry):
            base = pl.multiple_of(j * (K_UNROLL * K_BLOCK), K_UNROLL * K_BLOCK)
            for u in range(K_UNROLL):
                for qb in blocks:
                    qb.ring_step(base, u, K_UNROLL + ahead)
            return carry
        n_groups = (n_ring_blocks - ahead) // K_UNROLL
        lax.fori_loop(0, n_groups, body, 0)
        for blk in range(n_groups * K_UNROLL, n_ring_blocks):
            for qb in blocks:
                qb.ring_step(0, blk, n_ring_blocks - 1)
    for qb in blocks:
        qb.finish()


def _attn_call(qk, vt, lam_vecs, g_subln, *, n_heads, dk, n_lat, n_ctx, ctx_queries, lam_init):
    b, _, t, _ = qk.shape
    dv_ext = vt.shape[2]
    dv = dv_ext - V_PAD_ROWS
    if ctx_queries:
        tq, n_sub, n_steps, q_off, n_keys, k_off = n_ctx, 1, 1, n_lat // n_ctx, n_ctx, n_lat // n_ctx
        first_block, n_ring_blocks = (0, n_ctx), 0
    else:
        tq, n_sub, n_steps, q_off, n_keys, k_off = (
            Q_BLOCK, Q_PER_STEP, n_lat // (Q_BLOCK * Q_PER_STEP), 0, t, 0)
        first_block, n_ring_blocks = (n_lat, n_ctx), n_lat // K_BLOCK
    rows = tq * n_sub
    kern = functools.partial(_attn_kernel, dk=dk, tq=tq, first_block=first_block,
                             n_ring_blocks=n_ring_blocks, lam_init=lam_init)

    def score_bufs(n_keys):
        return [pltpu.VMEM((n_keys, 2 * tq), F32), pltpu.VMEM((1, 2 * tq), F32)]

    scratch_set = [
        pltpu.VMEM((2 * dk, 2 * tq), BF16),
        pltpu.VMEM((1, 2 * tq), F32),
        pltpu.VMEM((dv_ext, 2 * tq), F32),
        *score_bufs(first_block[1]),
        *((K_RING * score_bufs(K_BLOCK)) if n_ring_blocks else []),
    ]
    return pl.pallas_call(
        kern,
        grid=(b, n_heads, n_steps),
        in_specs=[
            pl.BlockSpec((None, None, rows, LANES), lambda bi, h, i: (bi, h, q_off + i, 0)),
            pl.BlockSpec((None, None, n_keys, LANES), lambda bi, h, i: (bi, n_heads + h, k_off, 0)),
            pl.BlockSpec((None, None, dv_ext, n_keys), lambda bi, h, i: (bi, h, 0, k_off)),
            pl.BlockSpec(lam_vecs.shape, lambda bi, h, i: (0, 0)),
            pl.BlockSpec((1, dv), lambda bi, h, i: (0, 0)),
        ],
        out_specs=pl.BlockSpec((None, rows, dv), lambda bi, h, i: (bi, i, h)),
        out_shape=jax.ShapeDtypeStruct((b, n_steps * rows, n_heads * dv), BF16),
        scratch_shapes=n_sub * scratch_set,
        compiler_params=_params(3),
        name="diff_attn_ctx" if ctx_queries else "diff_attn",
    )(qk, qk, vt, lam_vecs, g_subln)


def _attn_out_ffn_kernel(*refs, n_o, n_lat_blocks):
    o, _ = _token_rows(refs[:n_o], 0, n_lat_blocks)
    x, (mod_ref, gpost_ref, gfpre_ref, gfpost_ref, wo_ref, win_ref, wout_ref, out_ref) = (
        _token_rows(refs[n_o:], 8, n_lat_blocks))
    for g in range(out_ref.shape[0] // ROW_BLOCK):
        rows = slice(g * ROW_BLOCK, (g + 1) * ROW_BLOCK)
        y = jnp.dot(o[rows, :], wo_ref[...], preferred_element_type=F32)
        out_ref[rows, :] = _residual_ffn(x[rows, :], y, mod_ref, gpost_ref, gfpre_ref,
                                         gfpost_ref, win_ref, wout_ref)


def _attn_out_ffn_call(o_parts, streams, mods, g_post, g_fpre, g_fpost, wo, win, wout, *, n_lat):
    b, d = streams[0].shape[0], streams[0].shape[2]
    rows = sum(a.shape[1] for a in o_parts)
    tm = _row_block(rows, n_lat)
    n_lat_blocks = n_lat // tm
    return pl.pallas_call(
        functools.partial(_attn_out_ffn_kernel, n_o=len(o_parts), n_lat_blocks=n_lat_blocks),
        grid=(b, rows // tm),
        in_specs=[
            *_stream_specs(o_parts, tm, n_lat_blocks),
            *_stream_specs(streams, tm, n_lat_blocks),
            _mod_spec(d, n_lat_blocks),
            _const_spec((1, d)), _const_spec((1, d)), _const_spec((1, d)),
            _const_spec(wo.shape), _const_spec(win.shape), _const_spec(wout.shape),
        ],
        out_specs=pl.BlockSpec((None, tm, d), lambda bi, r: (bi, r, 0)),
        out_shape=jax.ShapeDtypeStruct((b, rows, d), F32),
        compiler_params=_params(2),
        name="attn_out_ffn",
    )(*o_parts, *streams, mods, g_post, g_fpre, g_fpost, wo, win, wout)


def _cmlp_ffn_kernel(x_ref, mod_ref, gpre_ref, wuv_ref, buv_ref, gsgu_ref, bsgu_ref,
                     wsp_ref, bsp_ref, wco_ref, gpost_ref, gfpre_ref, gfpost_ref,
                     win_ref, wout_ref, out_ref):
    groups, chunk, _ = wsp_ref.shape
    for rg in range(x_ref.shape[0] // ROW_BLOCK):
        rows = pl.ds(rg * ROW_BLOCK, ROW_BLOCK)
        x = x_ref[rows, :]
        h = _modulated(x, gpre_ref[...], mod_ref[0:1, :], mod_ref[1:2, :]).astype(BF16)
        z = jnp.dot(h, wuv_ref[...], preferred_element_type=F32) + buv_ref[...]
        z = 0.5 * z * (1.0 + lax.erf(z * math.sqrt(0.5)))
        w = z.shape[1] // 2
        u, v = z[:, :w], z[:, w:]
        mu = jnp.mean(v, axis=-1, keepdims=True)
        vc = v - mu
        var = jnp.mean(vc * vc, axis=-1, keepdims=True)
        vb = (vc * lax.rsqrt(var + EPS) * gsgu_ref[...] + bsgu_ref[...]).astype(BF16)

        gd = w // groups
        n_chunks = ROW_BLOCK // chunk
        mixed = []
        for g in range(groups):
            rhs = jnp.concatenate(
                [vb[n * chunk:(n + 1) * chunk, g * gd:(g + 1) * gd] for n in range(n_chunks)],
                axis=1)
            mixed.append(jnp.dot(wsp_ref[g], rhs, preferred_element_type=F32) + bsp_ref[g])
        sv = jnp.concatenate(
            [jnp.concatenate([mixed[g][:, n * gd:(n + 1) * gd] for g in range(groups)], axis=1)
             for n in range(n_chunks)], axis=0)
        y = jnp.dot((u * sv).astype(BF16), wco_ref[...], preferred_element_type=F32)
        out_ref[rows, :] = _residual_ffn(x, y, mod_ref, gpost_ref, gfpre_ref, gfpost_ref,
                                         win_ref, wout_ref)


def _cmlp_ffn_call(xt, mods, g_pre, wuv, buv, g_sgu, b_sgu, wsp, bsp, wco,
                   g_post, g_fpre, g_fpost, win, wout, *, n_lat):
    b, rows, d = xt.shape
    tm = _row_block(rows, n_lat)
    n_lat_blocks = n_lat // tm
    vec = lambda a: _const_spec((1, a.shape[-1]))
    return pl.pallas_call(
        _cmlp_ffn_kernel,
        grid=(b, rows // tm),
        in_specs=[
            pl.BlockSpec((None, tm, d), lambda bi, r: (bi, r, 0)),
            _mod_spec(d, n_lat_blocks),
            vec(g_pre), _const_spec(wuv.shape), vec(buv), vec(g_sgu), vec(b_sgu),
            _const_spec(wsp.shape), _const_spec(bsp.shape), _const_spec(wco.shape),
            vec(g_post), vec(g_fpre), vec(g_fpost),
            _const_spec(win.shape), _const_spec(wout.shape),
        ],
        out_specs=pl.BlockSpec((None, tm, d), lambda bi, r: (bi, r, 0)),
        out_shape=jax.ShapeDtypeStruct((b, rows, d), F32),
        compiler_params=_params(2),
        name="cmlp_ffn",
    )(xt, mods, g_pre, wuv, buv, g_sgu, b_sgu, wsp, bsp, wco, g_post, g_fpre, g_fpost, win, wout)


def _rope_tables(n_lat, n_ctx, dk):
    axis_dim = dk // 2
    rot = axis_dim // 2
    tok = np.arange(n_lat)
    row = (tok // ROPE_GRID_W).astype(np.float32)
    col = (tok % ROPE_GRID_W).astype(np.float32)
    inv_freq = (np.float32(ROPE_BASE)
                ** (-np.arange(0, axis_dim, 2, dtype=np.float32) / np.float32(axis_dim)))
    in_map = np.arange(LANES) % dk
    freq = inv_freq[in_map % rot].astype(np.float32)
    pos = np.where((in_map < axis_dim)[None, :], row[:, None], col[:, None])
    ang = (pos * freq[None, :]).astype(np.float32)
    first = ((in_map % axis_dim) < rot)[None, :]
    cos, sin = np.cos(ang), np.sin(ang)
    sa = np.where(first, -sin, 0.0)
    sb = np.where(first, 0.0, sin)
    pad = lambda a, v: jnp.asarray(np.concatenate(
        [a, np.full((n_ctx, LANES), v)], axis=0).astype(np.float32))
    return pad(cos, 1.0), pad(sa, 0.0), pad(sb, 0.0), rot


def kernel(x, c, ctx, c_ctx, w_ada, b_ada, g_mix_pre, g_mix_post, g_ffn_pre, g_ffn_post, w_qkv, w_attn_out, lam_q1, lam_k1, lam_q2, lam_k2, g_subln, w_uv, b_uv, g_sgu, b_sgu, w_spatial, b_spatial, w_cmlp_out, w_ffn_in, w_ffn_out):
    b, s, d = x.shape
    n_ctx = ctx.shape[1]
    depth = w_ada.shape[0]
    dk = lam_q1.shape[-1]
    dv = g_subln.shape[-1]
    n_heads = w_attn_out.shape[1] // dv
    qk_w = 2 * n_heads * dk * 2
    assert 2 * dk == LANES and dv == LANES
    assert K_UNROLL % K_RING == 0 and s % K_BLOCK == 0 and s % (Q_BLOCK * Q_PER_STEP) == 0 and s % n_ctx == 0
    assert s % ROW_BLOCK == 0 and n_ctx % ROW_BLOCK == 0 and b + 1 <= SUBLANES

    cond = jnp.zeros((SUBLANES, d), F32).at[:b].set(c).at[b].set(c_ctx)
    mod_all = _ada_call(cond, w_ada, b_ada)
    mod_lat = mod_all[:, :b].reshape(depth, b, 1, N_MOD, d)
    mod_ctx = jnp.broadcast_to(mod_all[:, b].reshape(depth, 1, 1, N_MOD, d), mod_lat.shape)
    mods = jnp.concatenate([mod_lat, mod_ctx], axis=2)

    cos, sa, sb, rot = _rope_tables(s, n_ctx, dk)
    last_attn = max(i for i in range(depth) if i % N_MIXERS == 0)
    row_vec = lambda a: a.reshape(1, -1)

    streams = (x, ctx)
    for i in range(depth):
        j = i // N_MIXERS
        ctx_live = i < last_attn
        win = w_ffn_in[i].astype(BF16)
        wout = w_ffn_out[i].astype(BF16)
        tail = (row_vec(g_mix_post[i]), row_vec(g_ffn_pre[i]), row_vec(g_ffn_post[i]))
        if i % N_MIXERS == 0:
            lam_init = 0.8 - 0.6 * math.exp(-0.3 * i)
            wqk = w_qkv[j][:, :qk_w].astype(BF16)
            wvt = w_qkv[j][:, qk_w:].T.astype(BF16)
            qk, vt = _qkv_call(streams, mods[i], row_vec(g_mix_pre[i]), wqk, wvt, cos, sa, sb,
                               n_lat=s, rot=rot, dk=dk, n_heads=n_heads)
            lam_vecs = jnp.stack([lam_q1[j], lam_k1[j], lam_q2[j], lam_k2[j]]).astype(F32)
            attend = functools.partial(_attn_call, qk, vt, lam_vecs, row_vec(g_subln[j]),
                                       n_heads=n_heads, dk=dk, n_lat=s, n_ctx=n_ctx,
                                       lam_init=lam_init)
            o_parts = (attend(ctx_queries=False),)
            if ctx_live:
                o_parts += (attend(ctx_queries=True),)
            streams = (_attn_out_ffn_call(o_parts, streams, mods[i], *tail, w_attn_out[j].astype(BF16),
                                          win, wout, n_lat=s),)
        else:
            xt, = streams
            if not ctx_live and xt.shape[1] != s:
                xt = xt[:, :s]
            streams = (_cmlp_ffn_call(
                xt, mods[i], row_vec(g_mix_pre[i]), w_uv[j].astype(BF16), row_vec(b_uv[j]),
                row_vec(g_sgu[j]), row_vec(b_sgu[j]), w_spatial[j].astype(BF16),
                b_spatial[j][:, :, None], w_cmlp_out[j].astype(BF16), *tail, win, wout,
                n_lat=s),)
    xt, = streams
    return xt[:, :s] if xt.shape[1] != s else xt
```

```python
import functools
import math

import jax
import jax.numpy as jnp
import numpy as np
from jax import lax
from jax.experimental import pallas as pl
from jax.experimental.pallas import tpu as pltpu

F32 = jnp.float32
BF16 = jnp.bfloat16

EPS = 1e-6
ROPE_GRID_W = 64
ROPE_BASE = 10000.0
N_MIXERS = 2
N_MOD = 6

LANES = 128
SUBLANES = 8
VMEM_LIMIT_BYTES = 56 << 20

ROW_BLOCK = 256
ROW_GROUPS = 2
Q_BLOCK = 256
Q_PER_STEP = 4
K_BLOCK = 1024
V_PAD_ROWS = 16
K_RING = 3
K_UNROLL = 6


def _const_spec(shape):
    return pl.BlockSpec(shape, lambda *_: (0,) * len(shape), pipeline_mode=pl.Buffered(1))


def _params(n_grid, flags=None):
    return pltpu.CompilerParams(
        dimension_semantics=("arbitrary",) * n_grid, vmem_limit_bytes=VMEM_LIMIT_BYTES,
        flags=flags)


def _rms(x, g):
    return x * lax.rsqrt(jnp.mean(x * x, axis=-1, keepdims=True) + EPS) * g


def _modulated(x, g, shift, scale):
    return _rms(x, g) * (1.0 + scale) + shift


def _ada_kernel(c_ref, w_ref, b_ref, o_ref):
    s = jax.nn.silu(c_ref[...])
    o_ref[...] = jnp.dot(s, w_ref[...], preferred_element_type=F32,
                         precision=lax.Precision.HIGHEST) + b_ref[...]


def _ada_call(cond, w_ada, b_ada):
    depth, d, n = w_ada.shape
    tn = 2048
    return pl.pallas_call(
        _ada_kernel,
        grid=(depth, n // tn),
        in_specs=[
            pl.BlockSpec(cond.shape, lambda l, k: (0, 0)),
            pl.BlockSpec((None, d, tn), lambda l, k: (l, 0, k)),
            pl.BlockSpec((None, 1, tn), lambda l, k: (l, 0, k)),
        ],
        out_specs=pl.BlockSpec((None, cond.shape[0], tn), lambda l, k: (l, 0, k)),
        out_shape=jax.ShapeDtypeStruct((depth, cond.shape[0], n), F32),
        compiler_params=_params(2),
        name="ada_mod",
    )(cond, w_ada, b_ada.reshape(depth, 1, n))


def _residual_ffn(x, y, mod_ref, gpost_ref, gfpre_ref, gfpost_ref, win_ref, wout_ref):
    x = x + mod_ref[2:3, :] * _rms(y, gpost_ref[...])
    h = _modulated(x, gfpre_ref[...], mod_ref[3:4, :], mod_ref[4:5, :]).astype(BF16)
    gu = jnp.dot(h, win_ref[...], preferred_element_type=F32)
    f = wout_ref.shape[0]
    a = (jax.nn.silu(gu[:, :f]) * gu[:, f:]).astype(BF16)
    y2 = jnp.dot(a, wout_ref[...], preferred_element_type=F32)
    return x + mod_ref[5:6, :] * _rms(y2, gfpost_ref[...])


def _row_block(rows, n_lat):
    wide = ROW_BLOCK * ROW_GROUPS
    return wide if rows == n_lat and rows % wide == 0 else ROW_BLOCK


def _mod_spec(d, n_lat_blocks):
    return pl.BlockSpec((None, None, N_MOD, d),
                        lambda b, r: (b, jnp.minimum(r // n_lat_blocks, 1), 0, 0))


def _stream_specs(streams, tm, n_lat_blocks):
    d = streams[0].shape[2]
    if len(streams) == 1:
        return [pl.BlockSpec((None, tm, d), lambda bi, r: (bi, r, 0))]
    return [pl.BlockSpec((None, tm, d), lambda bi, r: (bi, jnp.minimum(r, n_lat_blocks - 1), 0)),
            pl.BlockSpec((None, tm, d), lambda bi, r: (bi, jnp.maximum(r - n_lat_blocks, 0), 0))]


def _token_rows(refs, n_rest, n_lat_blocks):
    stream_refs, rest = refs[:len(refs) - n_rest], refs[len(refs) - n_rest:]
    if len(stream_refs) == 1:
        return stream_refs[0][...], rest
    lat_ref, ctx_ref = stream_refs
    return jnp.where(pl.program_id(1) >= n_lat_blocks, ctx_ref[...], lat_ref[...]), rest


def _qkv_kernel(*refs, rot, q_scale, n_lat_blocks):
    x, (mod_ref, g_ref, wqk_ref, wvt_ref, cos_ref, sa_ref, sb_ref, qk_ref, vt_ref) = (
        _token_rows(refs, 9, n_lat_blocks))
    h = _modulated(x, g_ref[...], mod_ref[0:1, :], mod_ref[1:2, :]).astype(BF16)
    qk = jnp.dot(h, wqk_ref[...], preferred_element_type=F32)
    cos, sa, sb = cos_ref[...], sa_ref[...], sb_ref[...]
    n_blk = qk.shape[1] // LANES
    for c in range(n_blk):
        blk = qk[:, c * LANES:(c + 1) * LANES]
        y = blk * cos + pltpu.roll(blk, LANES - rot, 1) * sa + pltpu.roll(blk, rot, 1) * sb
        if c < n_blk // 2:
            y = y * q_scale
        qk_ref[c] = y.astype(BF16)
    vt = lax.dot_general(wvt_ref[...], h, (((1,), (1,)), ((), ())), preferred_element_type=F32)
    n_heads, dv_ext, tm = vt_ref.shape
    dv = vt.shape[0] // n_heads
    pad_row = lax.broadcasted_iota(jnp.int32, (dv_ext - dv, tm), 0)
    pad = jnp.where(pad_row == 0, 1.0, 0.0).astype(BF16)
    for hd in range(n_heads):
        vt_ref[hd, :dv, :] = vt[hd * dv:(hd + 1) * dv, :].astype(BF16)
        vt_ref[hd, dv:, :] = pad


def _qkv_call(streams, mods, g_pre, wqk, wvt, cos, sa, sb, *, n_lat, rot, dk, n_heads):
    b, d = streams[0].shape[0], streams[0].shape[2]
    t = sum(a.shape[1] for a in streams)
    tm = ROW_BLOCK
    n_lat_blocks = n_lat // tm
    n_qk = wqk.shape[1] // LANES
    dv_ext = wvt.shape[0] // n_heads + V_PAD_ROWS
    return pl.pallas_call(
        functools.partial(_qkv_kernel, rot=rot, q_scale=dk ** -0.5 * math.log2(math.e),
                          n_lat_blocks=n_lat_blocks),
        grid=(b, t // tm),
        in_specs=[
            *_stream_specs(streams, tm, n_lat_blocks),
            _mod_spec(d, n_lat_blocks),
            _const_spec((1, d)),
            _const_spec(wqk.shape),
            _const_spec(wvt.shape),
            pl.BlockSpec((tm, LANES), lambda bi, r: (r, 0)),
            pl.BlockSpec((tm, LANES), lambda bi, r: (r, 0)),
            pl.BlockSpec((tm, LANES), lambda bi, r: (r, 0)),
        ],
        out_specs=[
            pl.BlockSpec((None, n_qk, tm, LANES), lambda bi, r: (bi, 0, r, 0)),
            pl.BlockSpec((None, n_heads, dv_ext, tm), lambda bi, r: (bi, 0, 0, r)),
        ],
        out_shape=[
            jax.ShapeDtypeStruct((b, n_qk, t, LANES), BF16),
            jax.ShapeDtypeStruct((b, n_heads, dv_ext, t), BF16),
        ],
        compiler_params=_params(2),
        name="qkv_rope",
    )(*streams, mods, g_pre, wqk, wvt, cos, sa, sb)


def _attn_kernel(q_ref, k_ref, vt_ref, lam_ref, gsub_ref, o_ref, *scratch,
                 dk, tq, first_block, n_ring_blocks, lam_init):
    lam = (jnp.exp(jnp.sum(lam_ref[0:1, :] * lam_ref[1:2, :], keepdims=True))
           - jnp.exp(jnp.sum(lam_ref[2:3, :] * lam_ref[3:4, :], keepdims=True)) + lam_init)
    dv = o_ref.shape[1]

    class QueryBlock:
        def __init__(self, rows, qst_ref, m_ref, acc_ref, *bufs):
            self.rows, self.qst_ref, self.m_ref, self.acc_ref = rows, qst_ref, m_ref, acc_ref
            self.first_buf = bufs[0:2]
            self.ring = tuple(bufs[i:i + 2] for i in range(2, len(bufs), 2))
            self.ahead = min(len(self.ring) - 1, n_ring_blocks)

        def start(self):
            qt = q_ref[self.rows, :].astype(F32).T
            row = lax.broadcasted_iota(jnp.int32, qt.shape, 0)
            self.qst_ref[:, :tq] = jnp.where(row < dk, qt, 0.0).astype(BF16)
            self.qst_ref[:, tq:] = jnp.where(row >= dk, qt, 0.0).astype(BF16)
            self.m_ref[...] = jnp.full(self.m_ref.shape, -jnp.inf, F32)
            self.acc_ref[...] = jnp.zeros(self.acc_ref.shape, F32)
            self.scores(*first_block, self.first_buf)
            for blk in range(self.ahead):
                self.scores(blk * K_BLOCK, K_BLOCK, self.ring[blk])
            self.absorb(*first_block, self.first_buf)

        def scores(self, start, size, buf):
            s_ref, mx_ref = buf
            s = jnp.dot(k_ref[pl.ds(start, size), :], self.qst_ref[...],
                        preferred_element_type=F32)
            s_ref[...] = s
            mx_ref[...] = jnp.max(s, axis=0, keepdims=True)

        def absorb(self, start, size, buf):
            s_ref, mx_ref = buf
            m_old = self.m_ref[...]
            m_new = jnp.maximum(m_old, mx_ref[...])
            alpha = jnp.exp2(m_old - m_new)
            p = jnp.exp2(s_ref[...] - m_new)
            pv = jnp.dot(vt_ref[:, pl.ds(start, size)], p.astype(BF16),
                         preferred_element_type=F32)
            self.acc_ref[...] = alpha * self.acc_ref[...] + pv
            self.m_ref[...] = m_new

        def ring_step(self, base, blk, last_scored):
            n = len(self.ring)
            if blk + self.ahead <= last_scored:
                self.scores(base + (blk + self.ahead) * K_BLOCK, K_BLOCK,
                            self.ring[(blk + self.ahead) % n])
            self.absorb(base + blk * K_BLOCK, K_BLOCK, self.ring[blk % n])

        def finish(self):
            acc = self.acc_ref[:dv, :]
            l = self.acc_ref[dv:dv + 1, :]
            o = acc[:, :tq] / l[:, :tq] - lam * (acc[:, tq:] / l[:, tq:])
            o_ref[self.rows, :] = (_rms(o.T, gsub_ref[...]) * (1.0 - lam_init)).astype(BF16)

    n_sub = q_ref.shape[0] // tq
    per_sub = len(scratch) // n_sub
    blocks = [QueryBlock(pl.ds(sub * tq, tq), *scratch[sub * per_sub:(sub + 1) * per_sub])
              for sub in range(n_sub)]
    for qb in blocks:
        qb.start()
    if n_ring_blocks:
        ahead = blocks[0].ahead

        def body(j, carry):
            base = pl.multiple_of(j * (K_UNROLL * K_BLOCK), K_UNROLL * K_BLOCK)
            for u in range(K_UNROLL):
                for qb in blocks:
                    qb.ring_step(base, u, K_UNROLL + ahead)
            return carry
        n_groups = (n_ring_blocks - ahead) // K_UNROLL
        lax.fori_loop(0, n_groups, body, 0)
        for blk in range(n_groups * K_UNROLL, n_ring_blocks):
            for qb in blocks:
                qb.ring_step(0, blk, n_ring_blocks - 1)
    for qb in blocks:
        qb.finish()


def _attn_call(qk, vt, lam_vecs, g_subln, *, n_heads, dk, n_lat, n_ctx, ctx_queries, lam_init):
    b, _, t, _ = qk.shape
    dv_ext = vt.shape[2]
    dv = dv_ext - V_PAD_ROWS
    if ctx_queries:
        tq, n_sub, n_steps, q_off, n_keys, k_off = n_ctx, 1, 1, n_lat // n_ctx, n_ctx, n_lat // n_ctx
        first_block, n_ring_blocks = (0, n_ctx), 0
    else:
        tq, n_sub, n_steps, q_off, n_keys, k_off = (
            Q_BLOCK, Q_PER_STEP, n_lat // (Q_BLOCK * Q_PER_STEP), 0, t, 0)
        first_block, n_ring_blocks = (n_lat, n_ctx), n_lat // K_BLOCK
    rows = tq * n_sub
    kern = functools.partial(_attn_kernel, dk=dk, tq=tq, first_block=first_block,
                             n_ring_blocks=n_ring_blocks, lam_init=lam_init)

    def score_bufs(n_keys):
        return [pltpu.VMEM((n_keys, 2 * tq), F32), pltpu.VMEM((1, 2 * tq), F32)]

    scratch_set = [
        pltpu.VMEM((2 * dk, 2 * tq), BF16),
        pltpu.VMEM((1, 2 * tq), F32),
        pltpu.VMEM((dv_ext, 2 * tq), F32),
        *score_bufs(first_block[1]),
        *((K_RING * score_bufs(K_BLOCK)) if n_ring_blocks else []),
    ]
    return pl.pallas_call(
        kern,
        grid=(b, n_heads, n_steps),
        in_specs=[
            pl.BlockSpec((None, None, rows, LANES), lambda bi, h, i: (bi, h, q_off + i, 0)),
            pl.BlockSpec((None, None, n_keys, LANES), lambda bi, h, i: (bi, n_heads + h, k_off, 0)),
            pl.BlockSpec((None, None, dv_ext, n_keys), lambda bi, h, i: (bi, h, 0, k_off)),
            pl.BlockSpec(lam_vecs.shape, lambda bi, h, i: (0, 0)),
            pl.BlockSpec((1, dv), lambda bi, h, i: (0, 0)),
        ],
        out_specs=pl.BlockSpec((None, rows, dv), lambda bi, h, i: (bi, i, h)),
        out_shape=jax.ShapeDtypeStruct((b, n_steps * rows, n_heads * dv), BF16),
        scratch_shapes=n_sub * scratch_set,
        compiler_params=_params(3),
        name="diff_attn_ctx" if ctx_queries else "diff_attn",
    )(qk, qk, vt, lam_vecs, g_subln)


def _attn_out_ffn_kernel(o_ref, *refs, n_lat_blocks):
    x, (mod_ref, gpost_ref, gfpre_ref, gfpost_ref, wo_ref, win_ref, wout_ref, out_ref) = (
        _token_rows(refs, 8, n_lat_blocks))
    for g in range(out_ref.shape[0] // ROW_BLOCK):
        rows = slice(g * ROW_BLOCK, (g + 1) * ROW_BLOCK)
        y = jnp.dot(o_ref[rows, :], wo_ref[...], preferred_element_type=F32)
        out_ref[rows, :] = _residual_ffn(x[rows, :], y, mod_ref, gpost_ref, gfpre_ref,
                                         gfpost_ref, win_ref, wout_ref)


def _attn_out_ffn_call(o, streams, mods, g_post, g_fpre, g_fpost, wo, win, wout, *, n_lat):
    b, rows, d = o.shape[0], o.shape[1], streams[0].shape[2]
    tm = _row_block(rows, n_lat)
    n_lat_blocks = n_lat // tm
    return pl.pallas_call(
        functools.partial(_attn_out_ffn_kernel, n_lat_blocks=n_lat_blocks),
        grid=(b, rows // tm),
        in_specs=[
            pl.BlockSpec((None, tm, o.shape[2]), lambda bi, r: (bi, r, 0)),
            *_stream_specs(streams, tm, n_lat_blocks),
            _mod_spec(d, n_lat_blocks),
            _const_spec((1, d)), _const_spec((1, d)), _const_spec((1, d)),
            _const_spec(wo.shape), _const_spec(win.shape), _const_spec(wout.shape),
        ],
        out_specs=pl.BlockSpec((None, tm, d), lambda bi, r: (bi, r, 0)),
        out_shape=jax.ShapeDtypeStruct((b, rows, d), F32),
        compiler_params=_params(2),
        name="attn_out_ffn",
    )(o, *streams, mods, g_post, g_fpre, g_fpost, wo, win, wout)


def _cmlp_ffn_kernel(x_ref, mod_ref, gpre_ref, wuv_ref, buv_ref, gsgu_ref, bsgu_ref,
                     wsp_ref, bsp_ref, wco_ref, gpost_ref, gfpre_ref, gfpost_ref,
                     win_ref, wout_ref, out_ref):
    groups, chunk, _ = wsp_ref.shape
    for rg in range(x_ref.shape[0] // ROW_BLOCK):
        rows = pl.ds(rg * ROW_BLOCK, ROW_BLOCK)
        x = x_ref[rows, :]
        h = _modulated(x, gpre_ref[...], mod_ref[0:1, :], mod_ref[1:2, :]).astype(BF16)
        w = wuv_ref.shape[1] // 2

        def gelu_proj(cols):
            z = jnp.dot(h, wuv_ref[:, cols], preferred_element_type=F32) + buv_ref[:, cols]
            return 0.5 * z * (1.0 + lax.erf(z * math.sqrt(0.5)))

        v = gelu_proj(slice(w, 2 * w))
        mu = jnp.mean(v, axis=-1, keepdims=True)
        vc = v - mu
        var = jnp.mean(vc * vc, axis=-1, keepdims=True)
        vb = (vc * lax.rsqrt(var + EPS) * gsgu_ref[...] + bsgu_ref[...]).astype(BF16)
        u = gelu_proj(slice(0, w))

        gd = w // groups
        n_chunks = ROW_BLOCK // chunk
        mixed = []
        for g in range(groups):
            rhs = jnp.concatenate(
                [vb[n * chunk:(n + 1) * chunk, g * gd:(g + 1) * gd] for n in range(n_chunks)],
                axis=1)
            mixed.append(jnp.dot(wsp_ref[g], rhs, preferred_element_type=F32) + bsp_ref[g])
        sv = jnp.concatenate(
            [jnp.concatenate([mixed[g][:, n * gd:(n + 1) * gd] for g in range(groups)], axis=1)
             for n in range(n_chunks)], axis=0)
        y = jnp.dot((u * sv).astype(BF16), wco_ref[...], preferred_element_type=F32)
        out_ref[rows, :] = _residual_ffn(x, y, mod_ref, gpost_ref, gfpre_ref, gfpost_ref,
                                         win_ref, wout_ref)


def _cmlp_ffn_call(xt, mods, g_pre, wuv, buv, g_sgu, b_sgu, wsp, bsp, wco,
                   g_post, g_fpre, g_fpost, win, wout, *, n_lat):
    b, rows, d = xt.shape
    tm = _row_block(rows, n_lat)
    n_lat_blocks = n_lat // tm
    vec = lambda a: _const_spec((1, a.shape[-1]))
    return pl.pallas_call(
        _cmlp_ffn_kernel,
        grid=(b, rows // tm),
        in_specs=[
            pl.BlockSpec((None, tm, d), lambda bi, r: (bi, r, 0)),
            _mod_spec(d, n_lat_blocks),
            vec(g_pre), _const_spec(wuv.shape), vec(buv), vec(g_sgu), vec(b_sgu),
            _const_spec(wsp.shape), _const_spec(bsp.shape), _const_spec(wco.shape),
            vec(g_post), vec(g_fpre), vec(g_fpost),
            _const_spec(win.shape), _const_spec(wout.shape),
        ],
        out_specs=pl.BlockSpec((None, tm, d), lambda bi, r: (bi, r, 0)),
        out_shape=jax.ShapeDtypeStruct((b, rows, d), F32),
        compiler_params=_params(2),
        name="cmlp_ffn",
    )(xt, mods, g_pre, wuv, buv, g_sgu, b_sgu, wsp, bsp, wco, g_post, g_fpre, g_fpost, win, wout)


def _rope_tables(n_lat, n_ctx, dk):
    axis_dim = dk // 2
    rot = axis_dim // 2
    tok = np.arange(n_lat)
    row = (tok // ROPE_GRID_W).astype(np.float32)
    col = (tok % ROPE_GRID_W).astype(np.float32)
    inv_freq = (np.float32(ROPE_BASE)
                ** (-np.arange(0, axis_dim, 2, dtype=np.float32) / np.float32(axis_dim)))
    in_map = np.arange(LANES) % dk
    freq = inv_freq[in_map % rot].astype(np.float32)
    pos = np.where((in_map < axis_dim)[None, :], row[:, None], col[:, None])
    ang = (pos * freq[None, :]).astype(np.float32)
    first = ((in_map % axis_dim) < rot)[None, :]
    cos, sin = np.cos(ang), np.sin(ang)
    sa = np.where(first, -sin, 0.0)
    sb = np.where(first, 0.0, sin)
    pad = lambda a, v: jnp.asarray(np.concatenate(
        [a, np.full((n_ctx, LANES), v)], axis=0).astype(np.float32))
    return pad(cos, 1.0), pad(sa, 0.0), pad(sb, 0.0), rot


def kernel(x, c, ctx, c_ctx, w_ada, b_ada, g_mix_pre, g_mix_post, g_ffn_pre, g_ffn_post, w_qkv, w_attn_out, lam_q1, lam_k1, lam_q2, lam_k2, g_subln, w_uv, b_uv, g_sgu, b_sgu, w_spatial, b_spatial, w_cmlp_out, w_ffn_in, w_ffn_out):
    b, s, d = x.shape
    n_ctx = ctx.shape[1]
    depth = w_ada.shape[0]
    dk = lam_q1.shape[-1]
    dv = g_subln.shape[-1]
    n_heads = w_attn_out.shape[1] // dv
    qk_w = 2 * n_heads * dk * 2
    assert 2 * dk == LANES and dv == LANES
    assert K_UNROLL % K_RING == 0 and s % K_BLOCK == 0 and s % (Q_BLOCK * Q_PER_STEP) == 0 and s % n_ctx == 0
    assert s % ROW_BLOCK == 0 and n_ctx % ROW_BLOCK == 0 and b + 1 <= SUBLANES

    cond = jnp.zeros((SUBLANES, d), F32).at[:b].set(c).at[b].set(c_ctx)
    mod_all = _ada_call(cond, w_ada, b_ada)
    mod_lat = mod_all[:, :b].reshape(depth, b, 1, N_MOD, d)
    mod_ctx = jnp.broadcast_to(mod_all[:, b].reshape(depth, 1, 1, N_MOD, d), mod_lat.shape)
    mods = jnp.concatenate([mod_lat, mod_ctx], axis=2)

    cos, sa, sb, rot = _rope_tables(s, n_ctx, dk)
    last_attn = max(i for i in range(depth) if i % N_MIXERS == 0)
    row_vec = lambda a: a.reshape(1, -1)

    streams = (x, ctx)
    for i in range(depth):
        j = i // N_MIXERS
        ctx_live = i < last_attn
        win = w_ffn_in[i].astype(BF16)
        wout = w_ffn_out[i].astype(BF16)
        tail = (row_vec(g_mix_post[i]), row_vec(g_ffn_pre[i]), row_vec(g_ffn_post[i]))
        if i % N_MIXERS == 0:
            lam_init = 0.8 - 0.6 * math.exp(-0.3 * i)
            wqk = w_qkv[j][:, :qk_w].astype(BF16)
            wvt = w_qkv[j][:, qk_w:].T.astype(BF16)
            qk, vt = _qkv_call(streams, mods[i], row_vec(g_mix_pre[i]), wqk, wvt, cos, sa, sb,
                               n_lat=s, rot=rot, dk=dk, n_heads=n_heads)
            lam_vecs = jnp.stack([lam_q1[j], lam_k1[j], lam_q2[j], lam_k2[j]]).astype(F32)
            attend = functools.partial(_attn_call, qk, vt, lam_vecs, row_vec(g_subln[j]),
                                       n_heads=n_heads, dk=dk, n_lat=s, n_ctx=n_ctx,
                                       lam_init=lam_init)
            o = attend(ctx_queries=False)
            if ctx_live:
                o = jnp.concatenate([o, attend(ctx_queries=True)], axis=1)
            streams = (_attn_out_ffn_call(o, streams, mods[i], *tail, w_attn_out[j].astype(BF16),
                                          win, wout, n_lat=s),)
        else:
            xt, = streams
            if not ctx_live and xt.shape[1] != s:
                xt = xt[:, :s]
            streams = (_cmlp_ffn_call(
                xt, mods[i], row_vec(g_mix_pre[i]), w_uv[j].astype(BF16), row_vec(b_uv[j]),
                row_vec(g_sgu[j]), row_vec(b_sgu[j]), w_spatial[j].astype(BF16),
                b_spatial[j][:, :, None], w_cmlp_out[j].astype(BF16), *tail, win, wout,
                n_lat=s),)
    xt, = streams
    return xt[:, :s] if xt.shape[1] != s else xt
```
